```python
import math
import jax, jax.numpy as jnp
from jax import lax
import numpy as np

D_MODEL = 4096
BATCH = 4
SEQ = 4096
DEPTH = 2

GRID_W = 64
CTX_LEN = 256
N_MIXERS = 2
CONV_WIDTH = 31
CONV_PAD = (CONV_WIDTH - 1) // 2
DA_HEAD_DIM = 128
DA_HEADS = D_MODEL // (2 * DA_HEAD_DIM)
DA_VDIM = 2 * DA_HEAD_DIM
DA_SCALE = DA_HEAD_DIM ** -0.5
Q_BLOCK = 128
ROPE_BASE = 10000.0
ROPE_FREQS = DA_HEAD_DIM // 4
PEER_HEADS = 8
PEER_NKEYS = 128
PEER_EXPERTS = PEER_NKEYS * PEER_NKEYS
PEER_QDIM = 256
PEER_HALF = PEER_QDIM // 2
PEER_TOPK = 16
PEER_TOK_BLOCK = 128
EPS = 1e-6

kernel_name = 'hybrid_conformer_diffattn_peer_dit'


def rmsnorm(x, g):
    xf = x.astype(jnp.float32)
    y = xf * lax.rsqrt(jnp.mean(xf * xf, axis=-1, keepdims=True) + EPS)
    return (y * g.astype(jnp.float32)).astype(x.dtype)


def layernorm(x, g, b):
    xf = x.astype(jnp.float32)
    mu = jnp.mean(xf, axis=-1, keepdims=True)
    xc = xf - mu
    y = xc * lax.rsqrt(jnp.mean(xc * xc, axis=-1, keepdims=True) + EPS)
    return (y * g.astype(jnp.float32) + b.astype(jnp.float32)).astype(x.dtype)


def adaln(cvec, w, b):
    m = jax.nn.silu(cvec) @ w + b
    return jnp.split(m, 6, axis=-1)


def modulate(h, shift, scale):
    return h * (1 + scale) + shift


def conv_module(h, glu_w, glu_b, dw_w, dw_b, ln_g, ln_b, pw_w, pw_b):
    a = h @ glu_w + glu_b
    a = a[..., :D_MODEL] * jax.nn.sigmoid(a[..., D_MODEL:])
    y = lax.conv_general_dilated(
        a, dw_w[:, None, :], window_strides=(1,), padding=[(CONV_PAD, CONV_PAD)],
        dimension_numbers=('NWC', 'WIO', 'NWC'), feature_group_count=D_MODEL) + dw_b
    y = jax.nn.silu(layernorm(y, ln_g, ln_b))
    return y @ pw_w + pw_b


def axial_rotary_tables(L, dtype):
    t = jnp.arange(L, dtype=jnp.int32)
    row = (t // GRID_W).astype(jnp.float32)
    col = (t % GRID_W).astype(jnp.float32)
    inv = ROPE_BASE ** (-jnp.arange(ROPE_FREQS, dtype=jnp.float32) / ROPE_FREQS)
    ar = row[:, None] * inv
    ac = col[:, None] * inv
    ang = jnp.concatenate([ar, ar, ac, ac], axis=-1)
    return jnp.cos(ang).astype(dtype), jnp.sin(ang).astype(dtype)


def apply_rotary(t, cos, sin):
    t4 = t.reshape(t.shape[:-1] + (2, 2, ROPE_FREQS))
    rot = jnp.concatenate([-t4[..., 1:, :], t4[..., :1, :]], axis=-2).reshape(t.shape)
    return t * cos[None, :, None, None, :] + rot * sin[None, :, None, None, :]


def diff_softmax_attend(q, k, v, lam):
    s = jnp.einsum('bqhmd,bkhmd->bhmqk', q, k).astype(jnp.float32) * DA_SCALE
    p = jax.nn.softmax(s, axis=-1)
    pd = (p[:, :, 0] - lam * p[:, :, 1]).astype(v.dtype)
    return jnp.einsum('bhqk,bkhe->bqhe', pd, v)


def diff_attention(h_lat, h_ctx, qkv_w, q_norm_g, k_norm_g, lam_q1, lam_k1, lam_q2, lam_k2,
                   subln_g, o_w, lam_init, with_ctx_out):
    B, L, _ = h_lat.shape

    def project(h):
        b, n, _ = h.shape
        q, k, v = jnp.split(h @ qkv_w, 3, axis=-1)
        q = rmsnorm(q.reshape(b, n, DA_HEADS, 2, DA_HEAD_DIM), q_norm_g)
        k = rmsnorm(k.reshape(b, n, DA_HEADS, 2, DA_HEAD_DIM), k_norm_g)
        return q, k, v.reshape(b, n, DA_HEADS, DA_VDIM)

    def merge_heads(o):
        b, n = o.shape[:2]
        o = rmsnorm(o, subln_g) * (1.0 - lam_init)
        return o.reshape(b, n, D_MODEL) @ o_w

    q_lat, k_lat, v_lat = project(h_lat)
    q_ctx, k_ctx, v_ctx = project(h_ctx)
    cos, sin = axial_rotary_tables(L, h_lat.dtype)
    q_lat = apply_rotary(q_lat, cos, sin)
    k_lat = apply_rotary(k_lat, cos, sin)

    f32 = jnp.float32
    lam = (jnp.exp(jnp.sum(lam_q1.astype(f32) * lam_k1.astype(f32)))
           - jnp.exp(jnp.sum(lam_q2.astype(f32) * lam_k2.astype(f32))) + lam_init)

    k_all = jnp.concatenate([k_lat, k_ctx], axis=1)
    v_all = jnp.concatenate([v_lat, v_ctx], axis=1)
    nb = L // Q_BLOCK
    qb = jnp.swapaxes(q_lat.reshape(B, nb, Q_BLOCK, DA_HEADS, 2, DA_HEAD_DIM), 0, 1)
    o_lat = lax.map(lambda qq: diff_softmax_attend(qq, k_all, v_all, lam), qb)
    o_lat = jnp.swapaxes(o_lat, 0, 1).reshape(B, L, DA_HEADS, DA_VDIM)
    y_lat = merge_heads(o_lat)
    y_ctx = merge_heads(diff_softmax_attend(q_ctx, k_ctx, v_ctx, lam)) if with_ctx_out else None
    return y_lat, y_ctx


def peer_ffn(h, wq, sub_k1, sub_k2, expert_u, expert_v):
    T = h.shape[0]
    HK = PEER_HEADS * PEER_TOPK
    q = (h @ wq).reshape(T, PEER_HEADS, 2, PEER_HALF)
    s1 = jnp.einsum('thd,nd->thn', q[:, :, 0], sub_k1)
    s2 = jnp.einsum('thd,nd->thn', q[:, :, 1], sub_k2)
    v1, i1 = lax.top_k(s1, PEER_TOPK)
    v2, i2 = lax.top_k(s2, PEER_TOPK)
    cand_s = (v1[..., :, None] + v2[..., None, :]).reshape(T, PEER_HEADS, PEER_TOPK * PEER_TOPK)
    cand_i = (i1[..., :, None] * PEER_NKEYS + i2[..., None, :]).reshape(T, PEER_HEADS, PEER_TOPK * PEER_TOPK)
    top_s, pos = lax.top_k(cand_s, PEER_TOPK)
    idx = jnp.take_along_axis(cand_i, pos, axis=-1).reshape(T, HK)
    gate = jax.nn.softmax(top_s.astype(jnp.float32), axis=-1).astype(h.dtype).reshape(T, HK)

    def apply_block(args):
        hb, ib, gb = args
        a = jax.nn.gelu(jnp.einsum('td,tkd->tk', hb, expert_u[ib])) * gb
        return jnp.einsum('tk,tkd->td', a, expert_v[ib])

    nb = T // PEER_TOK_BLOCK
    out = lax.map(apply_block, (h.reshape(nb, PEER_TOK_BLOCK, D_MODEL),
                                idx.reshape(nb, PEER_TOK_BLOCK, HK),
                                gate.reshape(nb, PEER_TOK_BLOCK, HK)))
    return out.reshape(T, D_MODEL)


def setup_inputs(seed: int = 0) -> dict:
    key = jax.random.key(seed)
    ks = iter(jax.random.split(key, 48))
    D = D_MODEL

    def nrm(shape, scale):
        return jax.random.normal(next(ks), shape, jnp.float32) * scale

    def gain(n):
        return 1.0 + nrm((n,), 0.02)

    inp = {}
    inp['x'] = nrm((BATCH, SEQ, D), 1.0)
    inp['c'] = nrm((BATCH, D), 1.0)
    inp['ctx'] = nrm((BATCH, CTX_LEN, D), 1.0)
    inp['c_ctx'] = nrm((D,), 1.0)
    inp['l0_ada_w'] = nrm((D, 6 * D), D ** -0.5)
    inp['l0_ada_b'] = nrm((6 * D,), 0.02)
    inp['l0_norm1_g'] = gain(D)
    inp['l0_glu_w'] = nrm((D, 2 * D), D ** -0.5)
    inp['l0_glu_b'] = nrm((2 * D,), 0.02)
    inp['l0_dw_w'] = nrm((CONV_WIDTH, D), CONV_WIDTH ** -0.5)
    inp['l0_dw_b'] = nrm((D,), 0.02)
    inp['l0_ln_g'] = gain(D)
    inp['l0_ln_b'] = nrm((D,), 0.02)
    inp['l0_pw_w'] = nrm((D, D), D ** -0.5)
    inp['l0_pw_b'] = nrm((D,), 0.02)
    inp['l0_norm2_g'] = gain(D)
    inp['l0_peer_wq'] = nrm((D, PEER_HEADS * PEER_QDIM), D ** -0.5)
    inp['l0_peer_k1'] = nrm((PEER_NKEYS, PEER_HALF), PEER_HALF ** -0.5)
    inp['l0_peer_k2'] = nrm((PEER_NKEYS, PEER_HALF), PEER_HALF ** -0.5)
    inp['l0_peer_u'] = nrm((PEER_EXPERTS, D), D ** -0.5)
    inp['l0_peer_v'] = nrm((PEER_EXPERTS, D), 0.5)
    inp['l1_ada_w'] = nrm((D, 6 * D), D ** -0.5)
    inp['l1_ada_b'] = nrm((6 * D,), 0.02)
    inp['l1_norm1_g'] = gain(D)
    inp['l1_qkv_w'] = nrm((D, 3 * D), D ** -0.5)
    inp['l1_q_norm_g'] = gain(DA_HEAD_DIM)
    inp['l1_k_norm_g'] = gain(DA_HEAD_DIM)
    inp['l1_lam_q1'] = nrm((DA_HEAD_DIM,), 0.1)
    inp['l1_lam_k1'] = nrm((DA_HEAD_DIM,), 0.1)
    inp['l1_lam_q2'] = nrm((DA_HEAD_DIM,), 0.1)
    inp['l1_lam_k2'] = nrm((DA_HEAD_DIM,), 0.1)
    inp['l1_subln_g'] = gain(DA_VDIM)
    inp['l1_o_w'] = nrm((D, D), D ** -0.5)
    inp['l1_norm2_g'] = gain(D)
    inp['l1_peer_wq'] = nrm((D, PEER_HEADS * PEER_QDIM), D ** -0.5)
    inp['l1_peer_k1'] = nrm((PEER_NKEYS, PEER_HALF), PEER_HALF ** -0.5)
    inp['l1_peer_k2'] = nrm((PEER_NKEYS, PEER_HALF), PEER_HALF ** -0.5)
    inp['l1_peer_u'] = nrm((PEER_EXPERTS, D), D ** -0.5)
    inp['l1_peer_v'] = nrm((PEER_EXPERTS, D), 0.5)
    return inp


def reference(x, c, ctx, c_ctx,
              l0_ada_w, l0_ada_b, l0_norm1_g, l0_glu_w, l0_glu_b, l0_dw_w, l0_dw_b,
              l0_ln_g, l0_ln_b, l0_pw_w, l0_pw_b, l0_norm2_g,
              l0_peer_wq, l0_peer_k1, l0_peer_k2, l0_peer_u, l0_peer_v,
              l1_ada_w, l1_ada_b, l1_norm1_g, l1_qkv_w, l1_q_norm_g, l1_k_norm_g,
              l1_lam_q1, l1_lam_k1, l1_lam_q2, l1_lam_k2, l1_subln_g, l1_o_w, l1_norm2_g,
              l1_peer_wq, l1_peer_k1, l1_peer_k2, l1_peer_u, l1_peer_v):
    B, L, D = x.shape
    C = ctx.shape[1]
    layers = [
        dict(ada_w=l0_ada_w, ada_b=l0_ada_b, norm1_g=l0_norm1_g, norm2_g=l0_norm2_g,
             glu_w=l0_glu_w, glu_b=l0_glu_b, dw_w=l0_dw_w, dw_b=l0_dw_b,
             ln_g=l0_ln_g, ln_b=l0_ln_b, pw_w=l0_pw_w, pw_b=l0_pw_b,
             peer=(l0_peer_wq, l0_peer_k1, l0_peer_k2, l0_peer_u, l0_peer_v)),
        dict(ada_w=l1_ada_w, ada_b=l1_ada_b, norm1_g=l1_norm1_g, norm2_g=l1_norm2_g,
             qkv_w=l1_qkv_w, q_norm_g=l1_q_norm_g, k_norm_g=l1_k_norm_g,
             lam_q1=l1_lam_q1, lam_k1=l1_lam_k1, lam_q2=l1_lam_q2, lam_k2=l1_lam_k2,
             subln_g=l1_subln_g, o_w=l1_o_w,
             peer=(l1_peer_wq, l1_peer_k1, l1_peer_k2, l1_peer_u, l1_peer_v)),
    ]
    for i in range(DEPTH):
        p = layers[i]
        last = i == DEPTH - 1
        sh1, sc1, g1, sh2, sc2, g2 = adaln(c[:, None, :], p['ada_w'], p['ada_b'])
        csh1, csc1, cg1, csh2, csc2, cg2 = adaln(c_ctx[None, None, :], p['ada_w'], p['ada_b'])
        h_lat = modulate(rmsnorm(x, p['norm1_g']), sh1, sc1)
        h_ctx = modulate(rmsnorm(ctx, p['norm1_g']), csh1, csc1)
        if i % N_MIXERS == 0:
            conv_args = (p['glu_w'], p['glu_b'], p['dw_w'], p['dw_b'], p['ln_g'], p['ln_b'], p['pw_w'], p['pw_b'])
            y_lat = conv_module(h_lat, *conv_args)
            y_ctx = None if last else conv_module(h_ctx, *conv_args)
        else:
            y_lat, y_ctx = diff_attention(
                h_lat, h_ctx, p['qkv_w'], p['q_norm_g'], p['k_norm_g'],
                p['lam_q1'], p['lam_k1'], p['lam_q2'], p['lam_k2'], p['subln_g'], p['o_w'],
                0.8 - 0.6 * math.exp(-0.3 * i), not last)
        x = x + g1 * y_lat
        h2_lat = modulate(rmsnorm(x, p['norm2_g']), sh2, sc2)
        if last:
            f = peer_ffn(h2_lat.reshape(B * L, D), *p['peer'])
            x = x + g2 * f.reshape(B, L, D)
        else:
            ctx = ctx + cg1 * y_ctx
            h2_ctx = modulate(rmsnorm(ctx, p['norm2_g']), csh2, csc2)
            f = peer_ffn(jnp.concatenate([h2_lat.reshape(B * L, D), h2_ctx.reshape(B * C, D)], axis=0), *p['peer'])
            x = x + g2 * f[:B * L].reshape(B, L, D)
            ctx = ctx + cg2 * f[B * L:].reshape(B, C, D)
    return x
```

```python
import functools
import math
from typing import Callable, NamedTuple

import jax
import jax.numpy as jnp
from jax import lax
from jax.experimental import pallas as pl
from jax.experimental.pallas import tpu as pltpu

F32 = jnp.float32
BF16 = jnp.bfloat16

GRID_W = 64
CONV_WIDTH = 31
CONV_PAD = (CONV_WIDTH - 1) // 2
DA_HEAD_DIM = 128
DA_VDIM = 2 * DA_HEAD_DIM
DA_SCALE = DA_HEAD_DIM ** -0.5
ROPE_BASE = 10000.0
ROPE_FREQS = DA_HEAD_DIM // 4
PEER_HEADS = 8
PEER_NKEYS = 128
PEER_QDIM = 256
PEER_HALF = PEER_QDIM // 2
PEER_TOPK = 16
EPS = 1e-6

V7X_LANES = 128
V7X_SUBLANES = 8
V7X_BF16_ROWS = 16
V7X_VMEM_BYTES = 64 * 1024 * 1024
VMEM_LIMIT = V7X_VMEM_BYTES - 8 * 1024 * 1024

ROW_CHUNK = 32
CONV_ROWS = 32
CONV_LANES = 256
HALO = 16


def _tile(n, pref):
    t = min(n, pref)
    while n % t:
        t //= 2
    return t


def _params(sem):
    return pltpu.CompilerParams(dimension_semantics=sem, vmem_limit_bytes=VMEM_LIMIT)


def _ada_kernel(c_ref, w_ref, b_ref, o_ref):
    c = c_ref[...]
    s = (c * jax.nn.sigmoid(c)).astype(BF16)
    o_ref[...] = jnp.dot(s, w_ref[...].astype(BF16), preferred_element_type=F32) + b_ref[...]


def _adaln(cvec, w, b):
    d, n = w.shape
    tn = _tile(n, 512)
    out = pl.pallas_call(
        _ada_kernel,
        grid=(n // tn,),
        in_specs=[pl.BlockSpec((8, d), lambda j: (0, 0)),
                  pl.BlockSpec((d, tn), lambda j: (0, j)),
                  pl.BlockSpec((1, tn), lambda j: (0, j))],
        out_specs=pl.BlockSpec((8, tn), lambda j: (0, j)),
        out_shape=jax.ShapeDtypeStruct((8, n), F32),
        compiler_params=_params(("arbitrary",)),
        name="adaln",
    )(cvec, w, b.reshape(1, n))
    return out.reshape(8, 1, n)


def _norm_mod_into(x_ref, g_ref, sh_ref, sc_ref, h_ref):
    rows = x_ref.shape[0]
    g = g_ref[...]
    sh = sh_ref[0]
    sc1 = 1.0 + sc_ref[0]

    def body(r, carry):
        sl = pl.ds(pl.multiple_of(r * ROW_CHUNK, ROW_CHUNK), ROW_CHUNK)
        x = x_ref[sl, :]
        y = x * lax.rsqrt(jnp.mean(x * x, axis=-1, keepdims=True) + EPS)
        h_ref[sl, :] = ((y * g) * sc1 + sh).astype(h_ref.dtype)
        return carry

    lax.fori_loop(0, rows // ROW_CHUNK, body, 0)


class ModRows(NamedTuple):
    row_of: Callable
    span: int


def _mod_spec(d, chunk, row_of):
    return pl.BlockSpec((1, 1, d), lambda i, j: (row_of(i), 0, chunk))


def _glu_kernel(x_ref, g_ref, sh_ref, sc_ref, wl_ref, wr_ref, bl_ref, br_ref, o_ref, h_ref):
    @pl.when(pl.program_id(1) == 0)
    def _():
        _norm_mod_into(x_ref, g_ref, sh_ref, sc_ref, h_ref)

    h = h_ref[...]
    left = jnp.dot(h, wl_ref[...], preferred_element_type=F32) + bl_ref[...]
    right = jnp.dot(h, wr_ref[...], preferred_element_type=F32) + br_ref[...]
    o_ref[...] = (left * jax.nn.sigmoid(right)).astype(o_ref.dtype)


def _glu(x, mod, rows, norm_g, w_bf, b):
    t, d = x.shape
    tm = _tile(rows.span, 512)
    row_of = rows.row_of
    tn = _tile(d, 512)
    nj = d // tn
    return pl.pallas_call(
        _glu_kernel,
        grid=(t // tm, nj),
        in_specs=[pl.BlockSpec((tm, d), lambda i, j: (i, 0)),
                  pl.BlockSpec((1, d), lambda i, j: (0, 0)),
                  _mod_spec(d, 0, lambda i: row_of(i * tm)), _mod_spec(d, 1, lambda i: row_of(i * tm)),
                  pl.BlockSpec((d, tn), lambda i, j: (0, j)),
                  pl.BlockSpec((d, tn), lambda i, j: (0, j + nj)),
                  pl.BlockSpec((1, tn), lambda i, j: (0, j)),
                  pl.BlockSpec((1, tn), lambda i, j: (0, j + nj))],
        out_specs=pl.BlockSpec((tm, tn), lambda i, j: (i, j)),
        out_shape=jax.ShapeDtypeStruct((t, d), BF16),
        scratch_shapes=[pltpu.VMEM((tm, d), BF16)],
        compiler_params=_params(("parallel", "arbitrary")),
        name="glu_proj",
    )(x, norm_g.reshape(1, d), mod, mod, w_bf, w_bf, b.reshape(1, 2 * d), b.reshape(1, 2 * d))


def _conv_pw_kernel(tiles_per_seq, ap_ref, am_ref, an_ref, dw_ref, dwb_ref, lng_ref, lnb_ref,
                    pw_ref, pwb_ref, x_ref, gate_ref, o_ref, abuf_ref, y_ref, h_ref):
    tm, d = am_ref.shape

    @pl.when(pl.program_id(1) == 0)
    def _():
        pos = pl.program_id(0) % tiles_per_seq
        prev = ap_ref[...].astype(F32)
        nxt = an_ref[...].astype(F32)
        abuf_ref[0:HALO, :] = jnp.where(pos == 0, 0.0, prev)
        abuf_ref[HALO:HALO + tm, :] = am_ref[...].astype(F32)
        abuf_ref[HALO + tm:HALO + tm + HALO, :] = jnp.where(pos == tiles_per_seq - 1, 0.0, nxt)

        first = HALO - CONV_PAD
        n_col = d // CONV_LANES

        def conv_body(it, carry):
            rb = pl.multiple_of((it // n_col) * CONV_ROWS, CONV_ROWS)
            cb = pl.multiple_of((it % n_col) * CONV_LANES, CONV_LANES)
            lanes = pl.ds(cb, CONV_LANES)
            win = abuf_ref[pl.ds(rb, CONV_ROWS + 2 * HALO), lanes]
            acc = jnp.zeros((CONV_ROWS, CONV_LANES), F32) + dwb_ref[:, lanes]
            for r in range(V7X_SUBLANES):
                shifted = win[r:r + CONV_ROWS + 2 * HALO - V7X_SUBLANES, :]
                for q in range(2 * HALO // V7X_SUBLANES):
                    k = q * V7X_SUBLANES + r - first
                    if 0 <= k < CONV_WIDTH:
                        tap = shifted[q * V7X_SUBLANES:q * V7X_SUBLANES + CONV_ROWS, :]
                        acc = acc + dw_ref[k:k + 1, lanes] * tap
            y_ref[pl.ds(rb, CONV_ROWS), lanes] = acc
            return carry

        lax.fori_loop(0, (tm // CONV_ROWS) * n_col, conv_body, 0)

        lng = lng_ref[...]
        lnb = lnb_ref[...]

        def ln_body(r, carry):
            sl = pl.ds(pl.multiple_of(r * ROW_CHUNK, ROW_CHUNK), ROW_CHUNK)
            y = y_ref[sl, :]
            mu = jnp.mean(y, axis=-1, keepdims=True)
            yc = y - mu
            z = yc * lax.rsqrt(jnp.mean(yc * yc, axis=-1, keepdims=True) + EPS)
            z = z * lng + lnb
            h_ref[sl, :] = (z * jax.nn.sigmoid(z)).astype(h_ref.dtype)
            return carry

        lax.fori_loop(0, tm // ROW_CHUNK, ln_body, 0)

    y = jnp.dot(h_ref[...], pw_ref[...], preferred_element_type=F32) + pwb_ref[...]
    o_ref[...] = x_ref[...] + gate_ref[0] * y


def _conv_pw(a, x, mod, rows, seq_len, dw_w, dw_b, ln_g, ln_b, pw_bf, pw_b):
    t, d = a.shape
    row_of = rows.row_of
    tm = _tile(seq_len, 512)
    tn = _tile(d, 512)
    nj = d // tn
    tiles_per_seq = seq_len // tm
    hb = tm // HALO
    n_hb = t // HALO
    return pl.pallas_call(
        functools.partial(_conv_pw_kernel, tiles_per_seq),
        grid=(t // tm, nj),
        in_specs=[pl.BlockSpec((HALO, d), lambda i, j: (jnp.maximum(i * hb - 1, 0), 0)),
                  pl.BlockSpec((tm, d), lambda i, j: (i, 0)),
                  pl.BlockSpec((HALO, d), lambda i, j: (jnp.minimum((i + 1) * hb, n_hb - 1), 0)),
                  pl.BlockSpec((CONV_WIDTH, d), lambda i, j: (0, 0)),
                  pl.BlockSpec((1, d), lambda i, j: (0, 0)),
                  pl.BlockSpec((1, d), lambda i, j: (0, 0)),
                  pl.BlockSpec((1, d), lambda i, j: (0, 0)),
                  pl.BlockSpec((d, tn), lambda i, j: (0, j)),
                  pl.BlockSpec((1, tn), lambda i, j: (0, j)),
                  pl.BlockSpec((tm, tn), lambda i, j: (i, j)),
                  pl.BlockSpec((1, 1, tn), lambda i, j: (row_of(i * tm), 0, 2 * nj + j))],
        out_specs=pl.BlockSpec((tm, tn), lambda i, j: (i, j)),
        out_shape=jax.ShapeDtypeStruct((t, d), F32),
        scratch_shapes=[pltpu.VMEM((tm + 2 * HALO, d), F32),
                        pltpu.VMEM((tm, d), F32),
                        pltpu.VMEM((tm, d), BF16)],
        compiler_params=_params(("parallel", "arbitrary")),
        name="conv_ln_pw",
    )(a, a, a, dw_w, dw_b.reshape(1, d), ln_g.reshape(1, d), ln_b.reshape(1, d),
      pw_bf, pw_b.reshape(1, d), x, mod)


def _top_distinct(s, val_ref, cnt_ref):
    for it in range(PEER_TOPK):
        m = jnp.max(s, axis=0, keepdims=True)
        eq = s == m
        val_ref[it:it + 1, :] = m
        cnt_ref[it:it + 1, :] = jnp.sum(jnp.where(eq, 1.0, 0.0), axis=0, keepdims=True)
        s = jnp.where(eq, -jnp.inf, s)


def _candidate_blocks(a_ref, b_ref, combine):
    half = PEER_TOPK // 2
    blocks = [combine(a_ref[0:1, :], b_ref[...]),
              combine(a_ref[half:PEER_TOPK, :], b_ref[0:1, :])]
    for a in range(1, half):
        blocks.append(combine(a_ref[a:a + 1, :], b_ref[0:half, :]))
    return jnp.concatenate(blocks, axis=0)


def _router_kernel(x_ref, g_ref, sh_ref, sc_ref, wq_ref, k1_ref, k2_ref,
                   ht_ref, s1_ref, s2_ref, aux_ref,
                   h_ref, v1_ref, n1_ref, v2_ref, n2_ref):
    tt, d = x_ref.shape

    @pl.when(pl.program_id(1) == 0)
    def _():
        _norm_mod_into(x_ref, g_ref, sh_ref, sc_ref, h_ref)

        def tr_body(c, carry):
            cb = pl.multiple_of(c * V7X_LANES, V7X_LANES)
            blk = h_ref[:, pl.ds(cb, V7X_LANES)].astype(F32)
            ht_ref[pl.ds(cb, V7X_LANES), :] = blk.T.astype(ht_ref.dtype)
            return carry

        lax.fori_loop(0, d // V7X_LANES, tr_body, 0)

    q = jnp.dot(h_ref[...], wq_ref[...], preferred_element_type=F32).astype(BF16)
    nt = (((1,), (1,)), ((), ()))
    s1 = lax.dot_general(k1_ref[...], q[:, :PEER_HALF], nt, preferred_element_type=F32)
    s2 = lax.dot_general(k2_ref[...], q[:, PEER_HALF:], nt, preferred_element_type=F32)
    s1_ref[0] = s1
    s2_ref[0] = s2

    _top_distinct(s1, v1_ref, n1_ref)
    _top_distinct(s2, v2_ref, n2_ref)
    cand = _candidate_blocks(v1_ref, v2_ref, lambda a, b: a + b)
    wgt = _candidate_blocks(n1_ref, n2_ref, lambda a, b: a * b)

    work = cand
    cnt = jnp.zeros((1, tt), F32)
    thr = jnp.zeros((1, tt), F32)
    for _ in range(PEER_TOPK):
        m = jnp.max(work, axis=0, keepdims=True)
        eq = work == m
        thr = jnp.where(cnt < PEER_TOPK, m, thr)
        cnt = cnt + jnp.sum(jnp.where(eq, wgt, 0.0), axis=0, keepdims=True)
        work = jnp.where(eq, -jnp.inf, work)

    m1 = v1_ref[0:1, :]
    m2 = v2_ref[0:1, :]
    z = jnp.sum(jnp.where(cand >= thr, wgt * jnp.exp(cand - (m1 + m2)), 0.0), axis=0, keepdims=True)
    aux_ref[0, 0:1, :] = thr
    aux_ref[0, 1:2, :] = m2
    aux_ref[0, 2:3, :] = m1 + jnp.log(z)
    aux_ref[0, 3:V7X_SUBLANES, :] = jnp.zeros((V7X_SUBLANES - 3, tt), F32)


def _router(x, mod, rows, norm_g, wq_bf, k1_bf, k2_bf):
    t, d = x.shape
    tt = _tile(rows.span, 512)
    row_of = rows.row_of
    nh = PEER_HEADS
    nk = PEER_NKEYS
    return pl.pallas_call(
        _router_kernel,
        grid=(t // tt, nh),
        in_specs=[pl.BlockSpec((tt, d), lambda i, j: (i, 0)),
                  pl.BlockSpec((1, d), lambda i, j: (0, 0)),
                  _mod_spec(d, 3, lambda i: row_of(i * tt)), _mod_spec(d, 4, lambda i: row_of(i * tt)),
                  pl.BlockSpec((d, PEER_QDIM), lambda i, j: (0, j)),
                  pl.BlockSpec((nk, PEER_HALF), lambda i, j: (0, 0)),
                  pl.BlockSpec((nk, PEER_HALF), lambda i, j: (0, 0))],
        out_specs=[pl.BlockSpec((d, tt), lambda i, j: (0, i)),
                   pl.BlockSpec((1, nk, tt), lambda i, j: (j, 0, i)),
                   pl.BlockSpec((1, nk, tt), lambda i, j: (j, 0, i)),
                   pl.BlockSpec((1, V7X_SUBLANES, tt), lambda i, j: (j, 0, i))],
        out_shape=[jax.ShapeDtypeStruct((d, t), BF16),
                   jax.ShapeDtypeStruct((nh, nk, t), F32),
                   jax.ShapeDtypeStruct((nh, nk, t), F32),
                   jax.ShapeDtypeStruct((nh, V7X_SUBLANES, t), F32)],
        scratch_shapes=[pltpu.VMEM((tt, d), BF16)] + [pltpu.VMEM((PEER_TOPK, tt), F32)] * 4,
        compiler_params=_params(("parallel", "arbitrary")),
        name="peer_router",
    )(x, norm_g.reshape(1, d), mod, mod, wq_bf, k1_bf, k2_bf)


def _expert_kernel(ht_ref, u_ref, v_ref, s1_ref, s2_ref, aux_ref, o_ref, e2_ref):
    te = u_ref.shape[0]
    nh = s1_ref.shape[0]
    e = pl.program_id(1)

    @pl.when(e == 0)
    def _():
        o_ref[...] = jnp.zeros(o_ref.shape, o_ref.dtype)
        for h in range(nh):
            e2_ref[h] = jnp.exp(s2_ref[h] - aux_ref[h, 1:2, :])

    act = jax.nn.gelu(jnp.dot(u_ref[...], ht_ref[...], preferred_element_type=F32))
    rows = []
    for r in range(te // PEER_NKEYS):
        n1 = e * (te // PEER_NKEYS) + r
        w = None
        for h in range(nh):
            s1row = s1_ref[h, pl.ds(n1, 1), :]
            g1row = jnp.exp(s1row - aux_ref[h, 2:3, :])
            sel = (s1row + s2_ref[h]) >= aux_ref[h, 0:1, :]
            term = jnp.where(sel, g1row * e2_ref[h], 0.0)
            w = term if w is None else w + term
        rows.append(w)
    wt = rows[0] if len(rows) == 1 else jnp.concatenate(rows, axis=0)
    p = (act * wt).astype(BF16)
    tn = (((0,), (0,)), ((), ()))
    o_ref[...] += lax.dot_general(p, v_ref[...], tn, preferred_element_type=F32)


def _experts(ht, s1, s2, aux, u_bf, v_bf):
    d, t = ht.shape
    ne = u_bf.shape[0]
    nh, nk, _ = s1.shape
    tt = _tile(t, 512)
    te = _tile(ne, 512)
    return pl.pallas_call(
        _expert_kernel,
        grid=(t // tt, ne // te),
        in_specs=[pl.BlockSpec((d, tt), lambda i, e: (0, i)),
                  pl.BlockSpec((te, d), lambda i, e: (e, 0)),
                  pl.BlockSpec((te, d), lambda i, e: (e, 0)),
                  pl.BlockSpec((nh, nk, tt), lambda i, e: (0, 0, i)),
                  pl.BlockSpec((nh, nk, tt), lambda i, e: (0, 0, i)),
                  pl.BlockSpec((nh, V7X_SUBLANES, tt), lambda i, e: (0, 0, i))],
        out_specs=pl.BlockSpec((tt, d), lambda i, e: (i, 0)),
        out_shape=jax.ShapeDtypeStruct((t, d), F32),
        scratch_shapes=[pltpu.VMEM((nh, nk, tt), F32)],
        compiler_params=_params(("parallel", "arbitrary")),
        name="peer_experts",
    )(ht, u_bf, v_bf, s1, s2, aux)


def _residual_kernel(x_ref, f_ref, gate_ref, o_ref):
    o_ref[...] = x_ref[...] + gate_ref[0] * f_ref[...]


def _residual(x, f, mod, rows, chunk):
    t, d = x.shape
    tm = _tile(rows.span, 256)
    row_of = rows.row_of
    return pl.pallas_call(
        _residual_kernel,
        grid=(t // tm,),
        in_specs=[pl.BlockSpec((tm, d), lambda i: (i, 0)),
                  pl.BlockSpec((tm, d), lambda i: (i, 0)),
                  pl.BlockSpec((1, 1, d), lambda i: (row_of(i * tm), 0, chunk))],
        out_specs=pl.BlockSpec((tm, d), lambda i: (i, 0)),
        out_shape=jax.ShapeDtypeStruct((t, d), F32),
        compiler_params=_params(("parallel",)),
        name="gated_residual",
    )(x, f, mod)


def _peer_layer(x, mod, rows, norm_g, wq_bf, k1_bf, k2_bf, u_bf, v_bf):
    ht, s1, s2, aux = _router(x, mod, rows, norm_g, wq_bf, k1_bf, k2_bf)
    f = _experts(ht, s1, s2, aux, u_bf, v_bf)
    return _residual(x, f, mod, rows, 5)


def _qkv_kernel(rotary, n_qk, x_ref, g_ref, sh_ref, sc_ref, w_ref, qg_ref, kg_ref,
                cos_ref, sa_ref, sb_ref, o_ref, h_ref):
    j = pl.program_id(1)

    @pl.when(j == 0)
    def _():
        _norm_mod_into(x_ref, g_ref, sh_ref, sc_ref, h_ref)

    acc = jnp.dot(h_ref[...], w_ref[...], preferred_element_type=F32)
    tn = acc.shape[1]

    def qk_norm(gain_ref, scale):
        cols = []
        for c in range(tn // DA_HEAD_DIM):
            t = acc[:, c * DA_HEAD_DIM:(c + 1) * DA_HEAD_DIM]
            t = t * lax.rsqrt(jnp.mean(t * t, axis=-1, keepdims=True) + EPS) * gain_ref[...]
            if rotary:
                t = (t * cos_ref[...]
                     + pltpu.roll(t, DA_HEAD_DIM - ROPE_FREQS, axis=1) * sa_ref[...]
                     + pltpu.roll(t, ROPE_FREQS, axis=1) * sb_ref[...])
            cols.append(t * scale if scale != 1.0 else t)
        o_ref[...] = jnp.concatenate(cols, axis=1).astype(o_ref.dtype)

    @pl.when(j < n_qk)
    def _():
        qk_norm(qg_ref, DA_SCALE)

    @pl.when(jnp.logical_and(j >= n_qk, j < 2 * n_qk))
    def _():
        qk_norm(kg_ref, 1.0)

    @pl.when(j >= 2 * n_qk)
    def _():
        o_ref[...] = acc.astype(o_ref.dtype)


def _rotary_tables(seq_len):
    t = jnp.arange(seq_len, dtype=jnp.int32)
    row = (t // GRID_W).astype(F32)
    col = (t % GRID_W).astype(F32)
    inv = ROPE_BASE ** (-jnp.arange(ROPE_FREQS, dtype=F32) / ROPE_FREQS)
    ar = row[:, None] * inv
    ac = col[:, None] * inv
    ang = jnp.concatenate([ar, ar, ac, ac], axis=-1)
    cos, sin = jnp.cos(ang), jnp.sin(ang)
    first_half = (jnp.arange(DA_HEAD_DIM) % (2 * ROPE_FREQS)) < ROPE_FREQS
    sin_a = jnp.where(first_half, -sin, 0.0)
    sin_b = jnp.where(first_half, 0.0, sin)
    return cos, sin_a, sin_b


def _qkv(x, mod, rows, seq_len, norm_g, w_bf, q_gain, k_gain, rotary):
    t, d = x.shape
    row_of = rows.row_of
    tm = _tile(seq_len, 512)
    tn = _tile(d, 512)
    n_qk = d // tn
    tiles_per_seq = seq_len // tm
    cos, sin_a, sin_b = _rotary_tables(seq_len)
    tab_spec = pl.BlockSpec((tm, DA_HEAD_DIM), lambda i, j: (i % tiles_per_seq, 0))
    return pl.pallas_call(
        functools.partial(_qkv_kernel, rotary, n_qk),
        grid=(t // tm, 3 * n_qk),
        in_specs=[pl.BlockSpec((tm, d), lambda i, j: (i, 0)),
                  pl.BlockSpec((1, d), lambda i, j: (0, 0)),
                  _mod_spec(d, 0, lambda i: row_of(i * tm)), _mod_spec(d, 1, lambda i: row_of(i * tm)),
                  pl.BlockSpec((d, tn), lambda i, j: (0, j)),
                  pl.BlockSpec((1, DA_HEAD_DIM), lambda i, j: (0, 0)),
                  pl.BlockSpec((1, DA_HEAD_DIM), lambda i, j: (0, 0)),
                  tab_spec, tab_spec, tab_spec],
        out_specs=pl.BlockSpec((tm, tn), lambda i, j: (i, j)),
        out_shape=jax.ShapeDtypeStruct((t, 3 * d), BF16),
        scratch_shapes=[pltpu.VMEM((tm, d), BF16)],
        compiler_params=_params(("parallel", "arbitrary")),
        name="qkv_proj",
    )(x, norm_g.reshape(1, d), mod, mod, w_bf, q_gain.reshape(1, -1), k_gain.reshape(1, -1),
      cos, sin_a, sin_b)


def _attn_kernel(lam_init, q_ref, kl_ref, vl_ref, kc_ref, vc_ref, lam_ref, sg_ref, o_ref):
    hd = DA_HEAD_DIM
    lq1, lk1, lq2, lk2 = (lam_ref[r:r + 1, :] for r in range(4))
    lam = (jnp.exp(jnp.sum(lq1 * lk1, axis=-1, keepdims=True))
           - jnp.exp(jnp.sum(lq2 * lk2, axis=-1, keepdims=True)) + lam_init)
    nt = (((1,), (1,)), ((), ()))
    v_lat = vl_ref[...]
    v_ctx = vc_ref[...]

    def attend(sub):
        q = q_ref[:, sub * hd:(sub + 1) * hd]
        s_lat = lax.dot_general(q, kl_ref[:, sub * hd:(sub + 1) * hd], nt, preferred_element_type=F32)
        s_ctx = lax.dot_general(q, kc_ref[:, sub * hd:(sub + 1) * hd], nt, preferred_element_type=F32)
        m = jnp.maximum(jnp.max(s_lat, axis=-1, keepdims=True), jnp.max(s_ctx, axis=-1, keepdims=True))
        p_lat = jnp.exp(s_lat - m)
        p_ctx = jnp.exp(s_ctx - m)
        z = jnp.sum(p_lat, axis=-1, keepdims=True) + jnp.sum(p_ctx, axis=-1, keepdims=True)
        o = (jnp.dot(p_lat.astype(BF16), v_lat, preferred_element_type=F32)
             + jnp.dot(p_ctx.astype(BF16), v_ctx, preferred_element_type=F32))
        return o / z

    o = attend(0) - lam * attend(1)
    o = o * lax.rsqrt(jnp.mean(o * o, axis=-1, keepdims=True) + EPS) * sg_ref[...]
    o_ref[...] = (o * (1.0 - lam_init)).astype(o_ref.dtype)


def _attention(qkv_lat, qkv_ctx, batch, lam_rows, subln_g, lam_init):
    t, d3 = qkv_lat.shape
    d = d3 // 3
    nh = d // DA_VDIM
    seq = t // batch
    ctx_len = qkv_ctx.shape[0] // batch
    tq = _tile(seq, 256)
    nq = seq // tq
    return pl.pallas_call(
        functools.partial(_attn_kernel, lam_init),
        grid=(batch, nh, nq),
        in_specs=[pl.BlockSpec((tq, DA_VDIM), lambda b, h, i: (b * nq + i, h)),
                  pl.BlockSpec((seq, DA_VDIM), lambda b, h, i: (b, nh + h)),
                  pl.BlockSpec((seq, DA_VDIM), lambda b, h, i: (b, 2 * nh + h)),
                  pl.BlockSpec((ctx_len, DA_VDIM), lambda b, h, i: (b, nh + h)),
                  pl.BlockSpec((ctx_len, DA_VDIM), lambda b, h, i: (b, 2 * nh + h)),
                  pl.BlockSpec((4, DA_HEAD_DIM), lambda b, h, i: (0, 0)),
                  pl.BlockSpec((1, DA_VDIM), lambda b, h, i: (0, 0))],
        out_specs=pl.BlockSpec((tq, DA_VDIM), lambda b, h, i: (b * nq + i, h)),
        out_shape=jax.ShapeDtypeStruct((t, d), BF16),
        compiler_params=_params(("parallel", "parallel", "arbitrary")),
        name="diff_attention",
    )(qkv_lat, qkv_lat, qkv_lat, qkv_ctx, qkv_ctx, lam_rows, subln_g.reshape(1, DA_VDIM))


def _oproj_kernel(a_ref, w_ref, x_ref, gate_ref, o_ref):
    y = jnp.dot(a_ref[...], w_ref[...], preferred_element_type=F32)
    o_ref[...] = x_ref[...] + gate_ref[0] * y


def _oproj(a, w_bf, x, mod, rows):
    t, d = x.shape
    tm = _tile(rows.span, 512)
    row_of = rows.row_of
    tn = _tile(d, 512)
    nj = d // tn
    return pl.pallas_call(
        _oproj_kernel,
        grid=(t // tm, nj),
        in_specs=[pl.BlockSpec((tm, d), lambda i, j: (i, 0)),
                  pl.BlockSpec((d, tn), lambda i, j: (0, j)),
                  pl.BlockSpec((tm, tn), lambda i, j: (i, j)),
                  pl.BlockSpec((1, 1, tn), lambda i, j: (row_of(i * tm), 0, 2 * nj + j))],
        out_specs=pl.BlockSpec((tm, tn), lambda i, j: (i, j)),
        out_shape=jax.ShapeDtypeStruct((t, d), F32),
        compiler_params=_params(("parallel", "arbitrary")),
        name="attn_out_proj",
    )(a, w_bf, x, mod)


def kernel(x, c, ctx, c_ctx, l0_ada_w, l0_ada_b, l0_norm1_g, l0_glu_w, l0_glu_b, l0_dw_w, l0_dw_b, l0_ln_g, l0_ln_b, l0_pw_w, l0_pw_b, l0_norm2_g, l0_peer_wq, l0_peer_k1, l0_peer_k2, l0_peer_u, l0_peer_v, l1_ada_w, l1_ada_b, l1_norm1_g, l1_qkv_w, l1_q_norm_g, l1_k_norm_g, l1_lam_q1, l1_lam_k1, l1_lam_q2, l1_lam_k2, l1_subln_g, l1_o_w, l1_norm2_g, l1_peer_wq, l1_peer_k1, l1_peer_k2, l1_peer_u, l1_peer_v):
    batch, seq, d = x.shape
    ctx_len = ctx.shape[1]
    assert seq % GRID_W == 0 and d % DA_VDIM == 0 and batch < 8
    xl = x.reshape(batch * seq, d)
    xc = ctx.reshape(batch * ctx_len, d)
    ctx_row = batch
    cvec = jnp.concatenate([c, c_ctx[None, :], jnp.zeros((8 - batch - 1, d), F32)], axis=0)

    lat_row = ModRows(lambda tok: tok // seq, seq)
    ctx_row_of = ModRows(lambda tok: ctx_row, batch * ctx_len)

    bf = lambda w: w.astype(BF16)

    mod0 = _adaln(cvec, l0_ada_w, l0_ada_b)
    glu_bf, pw_bf = bf(l0_glu_w), bf(l0_pw_w)
    peer0 = (bf(l0_peer_wq), bf(l0_peer_k1), bf(l0_peer_k2), bf(l0_peer_u), bf(l0_peer_v))
    conv_args = (l0_dw_w, l0_dw_b, l0_ln_g, l0_ln_b, pw_bf, l0_pw_b)

    def layer0(xs, row_of, seq_len):
        a = _glu(xs, mod0, row_of, l0_norm1_g, glu_bf, l0_glu_b)
        x1 = _conv_pw(a, xs, mod0, row_of, seq_len, *conv_args)
        return _peer_layer(x1, mod0, row_of, l0_norm2_g, *peer0)

    xl = layer0(xl, lat_row, seq)
    xc = layer0(xc, ctx_row_of, ctx_len)

    mod1 = _adaln(cvec, l1_ada_w, l1_ada_b)
    lam_init = 0.8 - 0.6 * math.exp(-0.3 * 1)
    qkv_bf = bf(l1_qkv_w)
    qkv_lat = _qkv(xl, mod1, lat_row, seq, l1_norm1_g, qkv_bf, l1_q_norm_g, l1_k_norm_g, True)
    qkv_ctx = _qkv(xc, mod1, ctx_row_of, ctx_len, l1_norm1_g, qkv_bf, l1_q_norm_g, l1_k_norm_g, False)
    lam_rows = jnp.stack([l1_lam_q1, l1_lam_k1, l1_lam_q2, l1_lam_k2], axis=0)
    o = _attention(qkv_lat, qkv_ctx, batch, lam_rows, l1_subln_g, lam_init)
    xl = _oproj(o, bf(l1_o_w), xl, mod1, lat_row)
    peer1 = (bf(l1_peer_wq), bf(l1_peer_k1), bf(l1_peer_k2), bf(l1_peer_u), bf(l1_peer_v))
    xl = _peer_layer(xl, mod1, lat_row, l1_norm2_g, *peer1)
    return xl.reshape(batch, seq, d)
```

```python
import functools
import math
from typing import Callable, NamedTuple

import jax
import jax.numpy as jnp
from jax import lax
from jax.experimental import pallas as pl
from jax.experimental.pallas import tpu as pltpu

F32 = jnp.float32
BF16 = jnp.bfloat16

GRID_W = 64
CONV_WIDTH = 31
CONV_PAD = (CONV_WIDTH - 1) // 2
DA_HEAD_DIM = 128
DA_VDIM = 2 * DA_HEAD_DIM
DA_SCALE = DA_HEAD_DIM ** -0.5
ROPE_BASE = 10000.0
ROPE_FREQS = DA_HEAD_DIM // 4
PEER_HEADS = 8
PEER_NKEYS = 128
PEER_QDIM = 256
PEER_HALF = PEER_QDIM // 2
PEER_TOPK = 16
EPS = 1e-6
LOG2E = math.log2(math.e)

V7X_LANES = 128
V7X_SUBLANES = 8
V7X_BF16_ROWS = 16
V7X_VMEM_BYTES = 64 * 1024 * 1024
VMEM_LIMIT = V7X_VMEM_BYTES - 8 * 1024 * 1024

ROW_CHUNK = 32
HALO = 16


def _tile(n, pref):
    t = min(n, pref)
    while n % t:
        t //= 2
    return t


def _params(sem):
    return pltpu.CompilerParams(dimension_semantics=sem, vmem_limit_bytes=VMEM_LIMIT)


def _ada_kernel(c_ref, w_ref, b_ref, o_ref):
    c = c_ref[...]
    s = (c * jax.nn.sigmoid(c)).astype(BF16)
    o_ref[...] = jnp.dot(s, w_ref[...].astype(BF16), preferred_element_type=F32) + b_ref[...]


def _adaln(cvec, w, b):
    d, n = w.shape
    tn = _tile(n, 512)
    out = pl.pallas_call(
        _ada_kernel,
        grid=(n // tn,),
        in_specs=[pl.BlockSpec((8, d), lambda j: (0, 0)),
                  pl.BlockSpec((d, tn), lambda j: (0, j)),
                  pl.BlockSpec((1, tn), lambda j: (0, j))],
        out_specs=pl.BlockSpec((8, tn), lambda j: (0, j)),
        out_shape=jax.ShapeDtypeStruct((8, n), F32),
        compiler_params=_params(("arbitrary",)),
        name="adaln",
    )(cvec, w, b.reshape(1, n))
    return out.reshape(8, 1, n)


def _norm_mod_into(x_ref, g_ref, sh_ref, sc_ref, h_ref):
    rows = x_ref.shape[0]
    g = g_ref[...]
    sh = sh_ref[0]
    sc1 = 1.0 + sc_ref[0]

    def body(r, carry):
        sl = pl.ds(pl.multiple_of(r * ROW_CHUNK, ROW_CHUNK), ROW_CHUNK)
        x = x_ref[sl, :]
        y = x * lax.rsqrt(jnp.mean(x * x, axis=-1, keepdims=True) + EPS)
        h_ref[sl, :] = ((y * g) * sc1 + sh).astype(h_ref.dtype)
        return carry

    lax.fori_loop(0, rows // ROW_CHUNK, body, 0)


class ModRows(NamedTuple):
    row_of: Callable
    span: int


def _mod_spec(d, chunk, row_of):
    return pl.BlockSpec((1, 1, d), lambda i, j: (row_of(i), 0, chunk))


def _glu_kernel(x_ref, g_ref, sh_ref, sc_ref, wl_ref, wr_ref, bl_ref, br_ref, o_ref, h_ref):
    @pl.when(pl.program_id(1) == 0)
    def _():
        _norm_mod_into(x_ref, g_ref, sh_ref, sc_ref, h_ref)

    h = h_ref[...]
    left = jnp.dot(h, wl_ref[...], preferred_element_type=F32) + bl_ref[...]
    right = jnp.dot(h, wr_ref[...], preferred_element_type=F32) + br_ref[...]
    o_ref[...] = (left * jax.nn.sigmoid(right)).astype(o_ref.dtype)


def _glu(x, mod, rows, norm_g, w_bf, b):
    t, d = x.shape
    tm = _tile(rows.span, 512)
    row_of = rows.row_of
    tn = _tile(d, 512)
    nj = d // tn
    return pl.pallas_call(
        _glu_kernel,
        grid=(t // tm, nj),
        in_specs=[pl.BlockSpec((tm, d), lambda i, j: (i, 0)),
                  pl.BlockSpec((1, d), lambda i, j: (0, 0)),
                  _mod_spec(d, 0, lambda i: row_of(i * tm)), _mod_spec(d, 1, lambda i: row_of(i * tm)),
                  pl.BlockSpec((d, tn), lambda i, j: (0, j)),
                  pl.BlockSpec((d, tn), lambda i, j: (0, j + nj)),
                  pl.BlockSpec((1, tn), lambda i, j: (0, j)),
                  pl.BlockSpec((1, tn), lambda i, j: (0, j + nj))],
        out_specs=pl.BlockSpec((tm, tn), lambda i, j: (i, j)),
        out_shape=jax.ShapeDtypeStruct((t, d), BF16),
        scratch_shapes=[pltpu.VMEM((tm, d), BF16)],
        compiler_params=_params(("parallel", "arbitrary")),
        name="glu_proj",
    )(x, norm_g.reshape(1, d), mod, mod, w_bf, w_bf, b.reshape(1, 2 * d), b.reshape(1, 2 * d))


def _conv_pw_kernel(tiles_per_seq, ap_ref, am_ref, an_ref, dw_ref, dwb_ref, lng_ref, lnb_ref,
                    pw_ref, pwb_ref, x_ref, gate_ref, o_ref, abuf_ref, y_ref, h_ref, sh_ref):
    tm, d = am_ref.shape

    @pl.when(pl.program_id(1) == 0)
    def _():
        pos = pl.program_id(0) % tiles_per_seq
        prev = ap_ref[...].astype(F32)
        nxt = an_ref[...].astype(F32)
        abuf_ref[0:HALO, :] = jnp.where(pos == 0, 0.0, prev)
        abuf_ref[HALO:HALO + tm, :] = am_ref[...].astype(F32)
        abuf_ref[HALO + tm:HALO + tm + HALO, :] = jnp.where(pos == tiles_per_seq - 1, 0.0, nxt)

        first = HALO - CONV_PAD
        sub = V7X_SUBLANES

        def conv_body(c, carry):
            lanes = pl.ds(pl.multiple_of(c * V7X_LANES, V7X_LANES), V7X_LANES)
            col = abuf_ref[:, lanes]
            n_keep = tm + 2 * HALO - sub
            for r in range(sub):
                sh_ref[r, 0:n_keep, :] = col[r:r + n_keep, :]
            taps = [dw_ref[k * sub:(k + 1) * sub, lanes] for k in range(CONV_WIDTH)]
            bias = jnp.broadcast_to(dwb_ref[:, lanes], (sub, V7X_LANES))
            for g in range(tm // sub):
                acc = bias
                for k in range(CONV_WIDTH):
                    q, r = divmod(first + k, sub)
                    acc = acc + taps[k] * sh_ref[r, (g + q) * sub:(g + q + 1) * sub, :]
                y_ref[g * sub:(g + 1) * sub, lanes] = acc
            return carry

        lax.fori_loop(0, d // V7X_LANES, conv_body, 0)

        lng = lng_ref[...]
        lnb = lnb_ref[...]

        def ln_body(r, carry):
            sl = pl.ds(pl.multiple_of(r * ROW_CHUNK, ROW_CHUNK), ROW_CHUNK)
            y = y_ref[sl, :]
            mu = jnp.mean(y, axis=-1, keepdims=True)
            yc = y - mu
            z = yc * lax.rsqrt(jnp.mean(yc * yc, axis=-1, keepdims=True) + EPS)
            z = z * lng + lnb
            h_ref[sl, :] = (z * jax.nn.sigmoid(z)).astype(h_ref.dtype)
            return carry

        lax.fori_loop(0, tm // ROW_CHUNK, ln_body, 0)

    y = jnp.dot(h_ref[...], pw_ref[...], preferred_element_type=F32) + pwb_ref[...]
    o_ref[...] = x_ref[...] + gate_ref[0] * y


def _conv_pw(a, x, mod, rows, seq_len, dw_w, dw_b, ln_g, ln_b, pw_bf, pw_b):
    t, d = a.shape
    row_of = rows.row_of
    tm = _tile(seq_len, 512)
    tn = _tile(d, 512)
    nj = d // tn
    tiles_per_seq = seq_len // tm
    hb = tm // HALO
    n_hb = t // HALO
    return pl.pallas_call(
        functools.partial(_conv_pw_kernel, tiles_per_seq),
        grid=(t // tm, nj),
        in_specs=[pl.BlockSpec((HALO, d), lambda i, j: (jnp.maximum(i * hb - 1, 0), 0)),
                  pl.BlockSpec((tm, d), lambda i, j: (i, 0)),
                  pl.BlockSpec((HALO, d), lambda i, j: (jnp.minimum((i + 1) * hb, n_hb - 1), 0)),
                  pl.BlockSpec((CONV_WIDTH * V7X_SUBLANES, d), lambda i, j: (0, 0)),
                  pl.BlockSpec((1, d), lambda i, j: (0, 0)),
                  pl.BlockSpec((1, d), lambda i, j: (0, 0)),
                  pl.BlockSpec((1, d), lambda i, j: (0, 0)),
                  pl.BlockSpec((d, tn), lambda i, j: (0, j)),
                  pl.BlockSpec((1, tn), lambda i, j: (0, j)),
                  pl.BlockSpec((tm, tn), lambda i, j: (i, j)),
                  pl.BlockSpec((1, 1, tn), lambda i, j: (row_of(i * tm), 0, 2 * nj + j))],
        out_specs=pl.BlockSpec((tm, tn), lambda i, j: (i, j)),
        out_shape=jax.ShapeDtypeStruct((t, d), F32),
        scratch_shapes=[pltpu.VMEM((tm + 2 * HALO, d), F32),
                        pltpu.VMEM((tm, d), F32),
                        pltpu.VMEM((tm, d), BF16),
                        pltpu.VMEM((V7X_SUBLANES, tm + 2 * HALO, V7X_LANES), F32)],
        compiler_params=_params(("parallel", "arbitrary")),
        name="conv_ln_pw",
    )(a, a, a, jnp.repeat(dw_w, V7X_SUBLANES, axis=0), dw_b.reshape(1, d), ln_g.reshape(1, d), ln_b.reshape(1, d),
      pw_bf, pw_b.reshape(1, d), x, mod)


def _merge_exchange_pairs(n):
    pairs = []
    p = 1
    while p < n:
        k = p
        while k >= 1:
            for j in range(k % p, n - k, 2 * k):
                for i in range(min(k, n - j - k)):
                    if (i + j) // (2 * p) == (i + j + k) // (2 * p):
                        pairs.append((i + j, i + j + k))
            k //= 2
        p *= 2
    return pairs


def _top_values(s, val_ref, cnt_ref):
    groups = s.shape[0] // V7X_SUBLANES
    assert groups == PEER_TOPK
    lists = [s[g * V7X_SUBLANES:(g + 1) * V7X_SUBLANES, :] for g in range(groups)]
    for i, j in _merge_exchange_pairs(groups):
        hi, lo = jnp.maximum(lists[i], lists[j]), jnp.minimum(lists[i], lists[j])
        lists[i], lists[j] = hi, lo
    for it in range(PEER_TOPK):
        m = jnp.max(lists[0], axis=0, keepdims=True)
        eq = lists[0] == m
        val_ref[it:it + 1, :] = m
        cnt_ref[it:it + 1, :] = jnp.sum(jnp.where(eq, 1.0, 0.0), axis=0, keepdims=True)
        for g in range(PEER_TOPK - it - 1):
            lists[g] = jnp.where(eq, lists[g + 1], lists[g])


def _candidate_blocks(a_ref, b_ref, combine):
    half = PEER_TOPK // 2
    blocks = [combine(a_ref[0:1, :], b_ref[...]),
              combine(a_ref[half:PEER_TOPK, :], b_ref[0:1, :])]
    for a in range(1, half):
        blocks.append(combine(a_ref[a:a + 1, :], b_ref[0:half, :]))
    return jnp.concatenate(blocks, axis=0)


def _pair_threshold(s1, s2, v1_ref, n1_ref, v2_ref, n2_ref):
    n = s1.shape[1]
    _top_values(s1, v1_ref, n1_ref)
    _top_values(s2, v2_ref, n2_ref)
    cand = _candidate_blocks(v1_ref, v2_ref, lambda a, b: a + b)
    wgt = _candidate_blocks(n1_ref, n2_ref, lambda a, b: a * b)
    work = cand
    cnt = jnp.zeros((1, n), F32)
    thr = jnp.zeros((1, n), F32)
    for _ in range(PEER_TOPK):
        m = jnp.max(work, axis=0, keepdims=True)
        eq = work == m
        thr = jnp.where(cnt < PEER_TOPK, m, thr)
        cnt = cnt + jnp.sum(jnp.where(eq, wgt, 0.0), axis=0, keepdims=True)
        work = jnp.where(eq, -jnp.inf, work)
    m1 = v1_ref[0:1, :]
    m2 = v2_ref[0:1, :]
    z = jnp.sum(jnp.where(cand >= thr, wgt * jnp.exp(cand - (m1 + m2)), 0.0), axis=0, keepdims=True)
    return thr, m1, m2, z


def _router_kernel(x_ref, g_ref, sh_ref, sc_ref, wq_ref, k1_ref, k2_ref,
                   ht_ref, s1_ref, s2_ref, aux_ref,
                   h_ref, v1_ref, n1_ref, v2_ref, n2_ref):
    tt, d = x_ref.shape

    @pl.when(pl.program_id(1) == 0)
    def _():
        _norm_mod_into(x_ref, g_ref, sh_ref, sc_ref, h_ref)

        def tr_body(c, carry):
            cb = pl.multiple_of(c * V7X_LANES, V7X_LANES)
            blk = h_ref[:, pl.ds(cb, V7X_LANES)].astype(F32)
            ht_ref[pl.ds(cb, V7X_LANES), :] = blk.T.astype(ht_ref.dtype)
            return carry

        lax.fori_loop(0, d // V7X_LANES, tr_body, 0)

    q = jnp.dot(h_ref[...], wq_ref[...], preferred_element_type=F32).astype(BF16)
    nt = (((1,), (1,)), ((), ()))
    s1 = lax.dot_general(k1_ref[...], q[:, :PEER_HALF], nt, preferred_element_type=F32)
    s2 = lax.dot_general(k2_ref[...], q[:, PEER_HALF:], nt, preferred_element_type=F32)
    s1_ref[0] = s1
    s2_ref[0] = s2

    thr, m1, m2, z = _pair_threshold(s1, s2, v1_ref, n1_ref, v2_ref, n2_ref)
    aux_ref[0, 0:1, :] = thr
    aux_ref[0, 1:2, :] = m2
    aux_ref[0, 2:3, :] = m1 + jnp.log(z)
    aux_ref[0, 3:V7X_SUBLANES, :] = jnp.zeros((V7X_SUBLANES - 3, tt), F32)


def _router(x, mod, rows, norm_g, wq_bf, k1_bf, k2_bf):
    t, d = x.shape
    tt = _tile(rows.span, 512)
    row_of = rows.row_of
    nh = PEER_HEADS
    nk = PEER_NKEYS
    return pl.pallas_call(
        _router_kernel,
        grid=(t // tt, nh),
        in_specs=[pl.BlockSpec((tt, d), lambda i, j: (i, 0)),
                  pl.BlockSpec((1, d), lambda i, j: (0, 0)),
                  _mod_spec(d, 3, lambda i: row_of(i * tt)), _mod_spec(d, 4, lambda i: row_of(i * tt)),
                  pl.BlockSpec((d, PEER_QDIM), lambda i, j: (0, j)),
                  pl.BlockSpec((nk, PEER_HALF), lambda i, j: (0, 0)),
                  pl.BlockSpec((nk, PEER_HALF), lambda i, j: (0, 0))],
        out_specs=[pl.BlockSpec((d, tt), lambda i, j: (0, i)),
                   pl.BlockSpec((1, nk, tt), lambda i, j: (j, 0, i)),
                   pl.BlockSpec((1, nk, tt), lambda i, j: (j, 0, i)),
                   pl.BlockSpec((1, V7X_SUBLANES, tt), lambda i, j: (j, 0, i))],
        out_shape=[jax.ShapeDtypeStruct((d, t), BF16),
                   jax.ShapeDtypeStruct((nh, nk, t), F32),
                   jax.ShapeDtypeStruct((nh, nk, t), F32),
                   jax.ShapeDtypeStruct((nh, V7X_SUBLANES, t), F32)],
        scratch_shapes=[pltpu.VMEM((tt, d), BF16)] + [pltpu.VMEM((PEER_TOPK, tt), F32)] * 4,
        compiler_params=_params(("parallel", "arbitrary")),
        name="peer_router",
    )(x, norm_g.reshape(1, d), mod, mod, wq_bf, k1_bf, k2_bf)


def _half_gate_rows(n1, s1_ref, s2_ref, e2_ref, aux_ref):
    w = None
    for h in range(s1_ref.shape[0]):
        s1row = s1_ref[h, pl.ds(n1, 1), :]
        g1row = 0.5 * jnp.exp(s1row - aux_ref[h, 2:3, :])
        sel = (s1row + s2_ref[h]) >= aux_ref[h, 0:1, :]
        term = jnp.where(sel, g1row * e2_ref[h], 0.0)
        w = term if w is None else w + term
    return w


def _expert_kernel(ht_ref, u_ref, v_ref, s1_ref, s2_ref, aux_ref, o_ref, e2_ref):
    te = u_ref.shape[0]
    nh = s1_ref.shape[0]
    e = pl.program_id(1)
    blocks = te // PEER_NKEYS

    @pl.when(e == 0)
    def _():
        o_ref[...] = jnp.zeros(o_ref.shape, o_ref.dtype)
        for h in range(nh):
            e2_ref[h] = jnp.exp(s2_ref[h] - aux_ref[h, 1:2, :])

    x = jnp.dot(u_ref[...], ht_ref[...], preferred_element_type=F32)
    c = math.sqrt(2.0 / math.pi)
    t = jnp.tanh(x * (c + (c * 0.044715) * (x * x)))
    half_w = jnp.concatenate(
        [_half_gate_rows(e * blocks + r, s1_ref, s2_ref, e2_ref, aux_ref) for r in range(blocks)], axis=0)
    p = ((x + x * t) * half_w).astype(BF16)
    tn = (((0,), (0,)), ((), ()))
    o_ref[...] += lax.dot_general(p, v_ref[...], tn, preferred_element_type=F32)


def _experts(ht, s1, s2, aux, u_bf, v_bf):
    d, t = ht.shape
    ne = u_bf.shape[0]
    nh, nk, _ = s1.shape
    tt = _tile(t, 512)
    te = _tile(ne, 512)
    return pl.pallas_call(
        _expert_kernel,
        grid=(t // tt, ne // te),
        in_specs=[pl.BlockSpec((d, tt), lambda i, e: (0, i)),
                  pl.BlockSpec((te, d), lambda i, e: (e, 0)),
                  pl.BlockSpec((te, d), lambda i, e: (e, 0)),
                  pl.BlockSpec((nh, nk, tt), lambda i, e: (0, 0, i)),
                  pl.BlockSpec((nh, nk, tt), lambda i, e: (0, 0, i)),
                  pl.BlockSpec((nh, V7X_SUBLANES, tt), lambda i, e: (0, 0, i))],
        out_specs=pl.BlockSpec((tt, d), lambda i, e: (i, 0)),
        out_shape=jax.ShapeDtypeStruct((t, d), F32),
        scratch_shapes=[pltpu.VMEM((nh, nk, tt), F32)],
        compiler_params=_params(("parallel", "arbitrary")),
        name="peer_experts",
    )(ht, u_bf, v_bf, s1, s2, aux)


def _residual_kernel(x_ref, f_ref, gate_ref, o_ref):
    o_ref[...] = x_ref[...] + gate_ref[0] * f_ref[...]


def _residual(x, f, mod, rows, chunk):
    t, d = x.shape
    tm = _tile(rows.span, 256)
    row_of = rows.row_of
    return pl.pallas_call(
        _residual_kernel,
        grid=(t // tm,),
        in_specs=[pl.BlockSpec((tm, d), lambda i: (i, 0)),
                  pl.BlockSpec((tm, d), lambda i: (i, 0)),
                  pl.BlockSpec((1, 1, d), lambda i: (row_of(i * tm), 0, chunk))],
        out_specs=pl.BlockSpec((tm, d), lambda i: (i, 0)),
        out_shape=jax.ShapeDtypeStruct((t, d), F32),
        compiler_params=_params(("parallel",)),
        name="gated_residual",
    )(x, f, mod)


def _peer_layer(x, mod, rows, norm_g, wq_bf, k1_bf, k2_bf, u_bf, v_bf):
    ht, s1, s2, aux = _router(x, mod, rows, norm_g, wq_bf, k1_bf, k2_bf)
    f = _experts(ht, s1, s2, aux, u_bf, v_bf)
    return _residual(x, f, mod, rows, 5)


def _qkv_kernel(rotary, n_qk, x_ref, g_ref, sh_ref, sc_ref, w_ref, qg_ref, kg_ref,
                cos_ref, sa_ref, sb_ref, o_ref, h_ref):
    j = pl.program_id(1)

    @pl.when(j == 0)
    def _():
        _norm_mod_into(x_ref, g_ref, sh_ref, sc_ref, h_ref)

    acc = jnp.dot(h_ref[...], w_ref[...], preferred_element_type=F32)
    tn = acc.shape[1]

    def qk_norm(gain_ref, scale):
        cols = []
        for c in range(tn // DA_HEAD_DIM):
            t = acc[:, c * DA_HEAD_DIM:(c + 1) * DA_HEAD_DIM]
            t = t * lax.rsqrt(jnp.mean(t * t, axis=-1, keepdims=True) + EPS) * gain_ref[...]
            if rotary:
                t = (t * cos_ref[...]
                     + pltpu.roll(t, DA_HEAD_DIM - ROPE_FREQS, axis=1) * sa_ref[...]
                     + pltpu.roll(t, ROPE_FREQS, axis=1) * sb_ref[...])
            cols.append(t * scale if scale != 1.0 else t)
        o_ref[...] = jnp.concatenate(cols, axis=1).astype(o_ref.dtype)

    @pl.when(j < n_qk)
    def _():
        qk_norm(qg_ref, DA_SCALE * LOG2E)

    @pl.when(jnp.logical_and(j >= n_qk, j < 2 * n_qk))
    def _():
        qk_norm(kg_ref, 1.0)

    @pl.when(j >= 2 * n_qk)
    def _():
        o_ref[...] = acc.astype(o_ref.dtype)


def _rotary_tables(seq_len):
    t = jnp.arange(seq_len, dtype=jnp.int32)
    row = (t // GRID_W).astype(F32)
    col = (t % GRID_W).astype(F32)
    inv = ROPE_BASE ** (-jnp.arange(ROPE_FREQS, dtype=F32) / ROPE_FREQS)
    ar = row[:, None] * inv
    ac = col[:, None] * inv
    ang = jnp.concatenate([ar, ar, ac, ac], axis=-1)
    cos, sin = jnp.cos(ang), jnp.sin(ang)
    first_half = (jnp.arange(DA_HEAD_DIM) % (2 * ROPE_FREQS)) < ROPE_FREQS
    sin_a = jnp.where(first_half, -sin, 0.0)
    sin_b = jnp.where(first_half, 0.0, sin)
    return cos, sin_a, sin_b


def _qkv(x, mod, rows, seq_len, norm_g, w_bf, q_gain, k_gain, rotary):
    t, d = x.shape
    row_of = rows.row_of
    tm = _tile(seq_len, 512)
    tn = _tile(d, 512)
    n_qk = d // tn
    tiles_per_seq = seq_len // tm
    cos, sin_a, sin_b = _rotary_tables(seq_len)
    tab_spec = pl.BlockSpec((tm, DA_HEAD_DIM), lambda i, j: (i % tiles_per_seq, 0))
    return pl.pallas_call(
        functools.partial(_qkv_kernel, rotary, n_qk),
        grid=(t // tm, 3 * n_qk),
        in_specs=[pl.BlockSpec((tm, d), lambda i, j: (i, 0)),
                  pl.BlockSpec((1, d), lambda i, j: (0, 0)),
                  _mod_spec(d, 0, lambda i: row_of(i * tm)), _mod_spec(d, 1, lambda i: row_of(i * tm)),
                  pl.BlockSpec((d, tn), lambda i, j: (0, j)),
                  pl.BlockSpec((1, DA_HEAD_DIM), lambda i, j: (0, 0)),
                  pl.BlockSpec((1, DA_HEAD_DIM), lambda i, j: (0, 0)),
                  tab_spec, tab_spec, tab_spec],
        out_specs=pl.BlockSpec((tm, tn), lambda i, j: (i, j)),
        out_shape=jax.ShapeDtypeStruct((t, 3 * d), BF16),
        scratch_shapes=[pltpu.VMEM((tm, d), BF16)],
        compiler_params=_params(("parallel", "arbitrary")),
        name="qkv_proj",
    )(x, norm_g.reshape(1, d), mod, mod, w_bf, q_gain.reshape(1, -1), k_gain.reshape(1, -1),
      cos, sin_a, sin_b)


def _attn_kernel(lam_init, q_ref, kl_ref, vl_ref, kc_ref, vc_ref, lam_ref, sg_ref, o_ref):
    hd = DA_HEAD_DIM
    lq1, lk1, lq2, lk2 = (lam_ref[r:r + 1, :] for r in range(4))
    lam = (jnp.exp(jnp.sum(lq1 * lk1, axis=-1, keepdims=True))
           - jnp.exp(jnp.sum(lq2 * lk2, axis=-1, keepdims=True)) + lam_init)
    nt = (((1,), (1,)), ((), ()))
    v_lat = vl_ref[...]
    v_ctx = vc_ref[...]

    def attend(sub):
        q = q_ref[:, sub * hd:(sub + 1) * hd]
        s_lat = lax.dot_general(q, kl_ref[:, sub * hd:(sub + 1) * hd], nt, preferred_element_type=F32)
        s_ctx = lax.dot_general(q, kc_ref[:, sub * hd:(sub + 1) * hd], nt, preferred_element_type=F32)
        m = jnp.maximum(jnp.max(s_lat, axis=-1, keepdims=True), jnp.max(s_ctx, axis=-1, keepdims=True))
        p_lat = jnp.exp2(s_lat - m)
        p_ctx = jnp.exp2(s_ctx - m)
        z = jnp.sum(p_lat, axis=-1, keepdims=True) + jnp.sum(p_ctx, axis=-1, keepdims=True)
        o = (jnp.dot(p_lat.astype(BF16), v_lat, preferred_element_type=F32)
             + jnp.dot(p_ctx.astype(BF16), v_ctx, preferred_element_type=F32))
        return o / z

    o = attend(0) - lam * attend(1)
    o = o * lax.rsqrt(jnp.mean(o * o, axis=-1, keepdims=True) + EPS) * sg_ref[...]
    o_ref[...] = (o * (1.0 - lam_init)).astype(o_ref.dtype)


def _attention(qkv_lat, qkv_ctx, batch, lam_rows, subln_g, lam_init):
    t, d3 = qkv_lat.shape
    d = d3 // 3
    nh = d // DA_VDIM
    seq = t // batch
    ctx_len = qkv_ctx.shape[0] // batch
    tq = _tile(seq, 512)
    nq = seq // tq
    return pl.pallas_call(
        functools.partial(_attn_kernel, lam_init),
        grid=(batch, nh, nq),
        in_specs=[pl.BlockSpec((tq, DA_VDIM), lambda b, h, i: (b * nq + i, h)),
                  pl.BlockSpec((seq, DA_VDIM), lambda b, h, i: (b, nh + h)),
                  pl.BlockSpec((seq, DA_VDIM), lambda b, h, i: (b, 2 * nh + h)),
                  pl.BlockSpec((ctx_len, DA_VDIM), lambda b, h, i: (b, nh + h)),
                  pl.BlockSpec((ctx_len, DA_VDIM), lambda b, h, i: (b, 2 * nh + h)),
                  pl.BlockSpec((4, DA_HEAD_DIM), lambda b, h, i: (0, 0)),
                  pl.BlockSpec((1, DA_VDIM), lambda b, h, i: (0, 0))],
        out_specs=pl.BlockSpec((tq, DA_VDIM), lambda b, h, i: (b * nq + i, h)),
        out_shape=jax.ShapeDtypeStruct((t, d), BF16),
        compiler_params=_params(("parallel", "parallel", "arbitrary")),
        name="diff_attention",
    )(qkv_lat, qkv_lat, qkv_lat, qkv_ctx, qkv_ctx, lam_rows, subln_g.reshape(1, DA_VDIM))


def _oproj_kernel(a_ref, w_ref, x_ref, gate_ref, o_ref):
    y = jnp.dot(a_ref[...], w_ref[...], preferred_element_type=F32)
    o_ref[...] = x_ref[...] + gate_ref[0] * y


def _oproj(a, w_bf, x, mod, rows):
    t, d = x.shape
    tm = _tile(rows.span, 1024)
    row_of = rows.row_of
    tn = _tile(d, 512)
    nj = d // tn
    return pl.pallas_call(
        _oproj_kernel,
        grid=(t // tm, nj),
        in_specs=[pl.BlockSpec((tm, d), lambda i, j: (i, 0)),
                  pl.BlockSpec((d, tn), lambda i, j: (0, j)),
                  pl.BlockSpec((tm, tn), lambda i, j: (i, j)),
                  pl.BlockSpec((1, 1, tn), lambda i, j: (row_of(i * tm), 0, 2 * nj + j))],
        out_specs=pl.BlockSpec((tm, tn), lambda i, j: (i, j)),
        out_shape=jax.ShapeDtypeStruct((t, d), F32),
        compiler_params=_params(("parallel", "arbitrary")),
        name="attn_out_proj",
    )(a, w_bf, x, mod)


def kernel(x, c, ctx, c_ctx, l0_ada_w, l0_ada_b, l0_norm1_g, l0_glu_w, l0_glu_b, l0_dw_w, l0_dw_b, l0_ln_g, l0_ln_b, l0_pw_w, l0_pw_b, l0_norm2_g, l0_peer_wq, l0_peer_k1, l0_peer_k2, l0_peer_u, l0_peer_v, l1_ada_w, l1_ada_b, l1_norm1_g, l1_qkv_w, l1_q_norm_g, l1_k_norm_g, l1_lam_q1, l1_lam_k1, l1_lam_q2, l1_lam_k2, l1_subln_g, l1_o_w, l1_norm2_g, l1_peer_wq, l1_peer_k1, l1_peer_k2, l1_peer_u, l1_peer_v):
    batch, seq, d = x.shape
    ctx_len = ctx.shape[1]
    assert seq % GRID_W == 0 and d % DA_VDIM == 0 and batch < 8
    xl = x.reshape(batch * seq, d)
    xc = ctx.reshape(batch * ctx_len, d)
    ctx_row = batch
    cvec = jnp.concatenate([c, c_ctx[None, :], jnp.zeros((8 - batch - 1, d), F32)], axis=0)

    lat_row = ModRows(lambda tok: tok // seq, seq)
    ctx_row_of = ModRows(lambda tok: ctx_row, batch * ctx_len)

    bf = lambda w: w.astype(BF16)

    mod0 = _adaln(cvec, l0_ada_w, l0_ada_b)
    glu_bf, pw_bf = bf(l0_glu_w), bf(l0_pw_w)
    peer0 = (bf(l0_peer_wq), bf(l0_peer_k1), bf(l0_peer_k2), bf(l0_peer_u), bf(l0_peer_v))
    conv_args = (l0_dw_w, l0_dw_b, l0_ln_g, l0_ln_b, pw_bf, l0_pw_b)

    def layer0(xs, row_of, seq_len):
        a = _glu(xs, mod0, row_of, l0_norm1_g, glu_bf, l0_glu_b)
        x1 = _conv_pw(a, xs, mod0, row_of, seq_len, *conv_args)
        return _peer_layer(x1, mod0, row_of, l0_norm2_g, *peer0)

    xl = layer0(xl, lat_row, seq)
    xc = layer0(xc, ctx_row_of, ctx_len)

    mod1 = _adaln(cvec, l1_ada_w, l1_ada_b)
    lam_init = 0.8 - 0.6 * math.exp(-0.3 * 1)
    qkv_bf = bf(l1_qkv_w)
    qkv_lat = _qkv(xl, mod1, lat_row, seq, l1_norm1_g, qkv_bf, l1_q_norm_g, l1_k_norm_g, True)
    qkv_ctx = _qkv(xc, mod1, ctx_row_of, ctx_len, l1_norm1_g, qkv_bf, l1_q_norm_g, l1_k_norm_g, False)
    lam_rows = jnp.stack([l1_lam_q1, l1_lam_k1, l1_lam_q2, l1_lam_k2], axis=0)
    o = _attention(qkv_lat, qkv_ctx, batch, lam_rows, l1_subln_g, lam_init)
    xl = _oproj(o, bf(l1_o_w), xl, mod1, lat_row)
    peer1 = (bf(l1_peer_wq), bf(l1_peer_k1), bf(l1_peer_k2), bf(l1_peer_u), bf(l1_peer_v))
    xl = _peer_layer(xl, mod1, lat_row, l1_norm2_g, *peer1)
    return xl.reshape(batch, seq, d)
```

```python
import functools
import math
from typing import Callable, NamedTuple

import jax
import jax.numpy as jnp
from jax import lax
from jax.experimental import pallas as pl
from jax.experimental.pallas import tpu as pltpu

F32 = jnp.float32
BF16 = jnp.bfloat16

GRID_W = 64
CONV_WIDTH = 31
CONV_PAD = (CONV_WIDTH - 1) // 2
DA_HEAD_DIM = 128
DA_VDIM = 2 * DA_HEAD_DIM
DA_SCALE = DA_HEAD_DIM ** -0.5
ROPE_BASE = 10000.0
ROPE_FREQS = DA_HEAD_DIM // 4
PEER_HEADS = 8
PEER_NKEYS = 128
PEER_QDIM = 256
PEER_HALF = PEER_QDIM // 2
PEER_TOPK = 16
EPS = 1e-6
LOG2E = math.log2(math.e)

V7X_LANES = 128
V7X_SUBLANES = 8
V7X_BF16_ROWS = 16
V7X_VMEM_BYTES = 64 * 1024 * 1024
VMEM_LIMIT = V7X_VMEM_BYTES - 8 * 1024 * 1024

ROW_CHUNK = 64
COL_CHUNK = 512
HALO = 16


def _tile(n, pref):
    t = min(n, pref)
    while n % t:
        t //= 2
    return t


def _params(sem):
    return pltpu.CompilerParams(dimension_semantics=sem, vmem_limit_bytes=VMEM_LIMIT)


def _ada_kernel(c_ref, w_ref, b_ref, o_ref):
    c = c_ref[...]
    s = (c * jax.nn.sigmoid(c)).astype(BF16)
    o_ref[...] = jnp.dot(s, w_ref[...].astype(BF16), preferred_element_type=F32) + b_ref[...]


def _adaln(cvec, w, b):
    d, n = w.shape
    tn = _tile(n, 512)
    out = pl.pallas_call(
        _ada_kernel,
        grid=(n // tn,),
        in_specs=[pl.BlockSpec((8, d), lambda j: (0, 0)),
                  pl.BlockSpec((d, tn), lambda j: (0, j)),
                  pl.BlockSpec((1, tn), lambda j: (0, j))],
        out_specs=pl.BlockSpec((8, tn), lambda j: (0, j)),
        out_shape=jax.ShapeDtypeStruct((8, n), F32),
        compiler_params=_params(("arbitrary",)),
        name="adaln",
    )(cvec, w, b.reshape(1, n))
    return out.reshape(8, 1, n)


def _norm_mod_into(x_ref, g_ref, sh_ref, sc_ref, h_ref):
    rows, d = x_ref.shape
    sub_rows = V7X_BF16_ROWS
    cc = min(d, COL_CHUNK)

    def body(r, carry):
        for s in range(ROW_CHUNK // sub_rows):
            sl = pl.ds(pl.multiple_of(r * ROW_CHUNK + s * sub_rows, sub_rows), sub_rows)
            part = None
            for c in range(d // cc):
                xs = x_ref[sl, c * cc:(c + 1) * cc]
                sq = xs * xs
                for lb in range(cc // V7X_LANES):
                    piece = sq[:, lb * V7X_LANES:(lb + 1) * V7X_LANES]
                    part = piece if part is None else part + piece
            inv = lax.rsqrt(jnp.sum(part, axis=-1, keepdims=True) * (1.0 / d) + EPS)
            for c in range(d // cc):
                cols = slice(c * cc, (c + 1) * cc)
                y = x_ref[sl, cols] * inv
                h_ref[sl, cols] = ((y * g_ref[:, cols]) * (1.0 + sc_ref[0, :, cols]) + sh_ref[0, :, cols]
                                   ).astype(h_ref.dtype)
        return carry

    lax.fori_loop(0, rows // ROW_CHUNK, body, 0)


class ModRows(NamedTuple):
    row_of: Callable
    span: int


def _mod_spec(d, chunk, row_of):
    return pl.BlockSpec((1, 1, d), lambda i, j: (row_of(i), 0, chunk))


def _glu_kernel(x_ref, g_ref, sh_ref, sc_ref, wl_ref, wr_ref, bl_ref, br_ref, o_ref, h_ref):
    @pl.when(pl.program_id(1) == 0)
    def _():
        _norm_mod_into(x_ref, g_ref, sh_ref, sc_ref, h_ref)

    h = h_ref[...]
    left = jnp.dot(h, wl_ref[...], preferred_element_type=F32) + bl_ref[...]
    right = jnp.dot(h, wr_ref[...], preferred_element_type=F32) + br_ref[...]
    o_ref[...] = (left * jax.nn.sigmoid(right)).astype(o_ref.dtype)


def _glu(x, mod, rows, norm_g, w_bf, b):
    t, d = x.shape
    tm = _tile(rows.span, 512)
    row_of = rows.row_of
    tn = _tile(d, 512)
    nj = d // tn
    return pl.pallas_call(
        _glu_kernel,
        grid=(t // tm, nj),
        in_specs=[pl.BlockSpec((tm, d), lambda i, j: (i, 0)),
                  pl.BlockSpec((1, d), lambda i, j: (0, 0)),
                  _mod_spec(d, 0, lambda i: row_of(i * tm)), _mod_spec(d, 1, lambda i: row_of(i * tm)),
                  pl.BlockSpec((d, tn), lambda i, j: (0, j)),
                  pl.BlockSpec((d, tn), lambda i, j: (0, j + nj)),
                  pl.BlockSpec((1, tn), lambda i, j: (0, j)),
                  pl.BlockSpec((1, tn), lambda i, j: (0, j + nj))],
        out_specs=pl.BlockSpec((tm, tn), lambda i, j: (i, j)),
        out_shape=jax.ShapeDtypeStruct((t, d), BF16),
        scratch_shapes=[pltpu.VMEM((tm, d), BF16)],
        compiler_params=_params(("parallel", "arbitrary")),
        name="glu_proj",
    )(x, norm_g.reshape(1, d), mod, mod, w_bf, w_bf, b.reshape(1, 2 * d), b.reshape(1, 2 * d))


def _conv_pw_kernel(tiles_per_seq, ap_ref, am_ref, an_ref, dw_ref, dwb_ref, lng_ref, lnb_ref,
                    pw_ref, pwb_ref, x_ref, gate_ref, o_ref, abuf_ref, y_ref, h_ref, sh_ref):
    tm, d = am_ref.shape

    @pl.when(pl.program_id(1) == 0)
    def _():
        pos = pl.program_id(0) % tiles_per_seq
        prev = ap_ref[...].astype(F32)
        nxt = an_ref[...].astype(F32)
        abuf_ref[0:HALO, :] = jnp.where(pos == 0, 0.0, prev)
        abuf_ref[HALO:HALO + tm, :] = am_ref[...].astype(F32)
        abuf_ref[HALO + tm:HALO + tm + HALO, :] = jnp.where(pos == tiles_per_seq - 1, 0.0, nxt)

        first = HALO - CONV_PAD
        sub = V7X_SUBLANES

        def conv_body(c, carry):
            lanes = pl.ds(pl.multiple_of(c * V7X_LANES, V7X_LANES), V7X_LANES)
            col = abuf_ref[:, lanes]
            n_keep = tm + 2 * HALO - sub
            for r in range(sub):
                sh_ref[r, 0:n_keep, :] = col[r:r + n_keep, :]
            taps = [dw_ref[k * sub:(k + 1) * sub, lanes] for k in range(CONV_WIDTH)]
            bias = jnp.broadcast_to(dwb_ref[:, lanes], (sub, V7X_LANES))
            for g in range(tm // sub):
                acc = bias
                for k in range(CONV_WIDTH):
                    q, r = divmod(first + k, sub)
                    acc = acc + taps[k] * sh_ref[r, (g + q) * sub:(g + q + 1) * sub, :]
                y_ref[g * sub:(g + 1) * sub, lanes] = acc
            return carry

        lax.fori_loop(0, d // V7X_LANES, conv_body, 0)

        sub_rows = V7X_BF16_ROWS
        cc = min(d, COL_CHUNK)

        def row_sum(sl, fn):
            part = None
            for c in range(d // cc):
                blk = fn(y_ref[sl, c * cc:(c + 1) * cc])
                for lb in range(cc // V7X_LANES):
                    piece = blk[:, lb * V7X_LANES:(lb + 1) * V7X_LANES]
                    part = piece if part is None else part + piece
            return jnp.sum(part, axis=-1, keepdims=True)

        def ln_body(r, carry):
            for s in range(ROW_CHUNK // sub_rows):
                sl = pl.ds(pl.multiple_of(r * ROW_CHUNK + s * sub_rows, sub_rows), sub_rows)
                mu = row_sum(sl, lambda b: b) * (1.0 / d)
                inv = lax.rsqrt(row_sum(sl, lambda b: (b - mu) * (b - mu)) * (1.0 / d) + EPS)
                for c in range(d // cc):
                    cols = slice(c * cc, (c + 1) * cc)
                    z = ((y_ref[sl, cols] - mu) * inv) * lng_ref[:, cols] + lnb_ref[:, cols]
                    h_ref[sl, cols] = (z * jax.nn.sigmoid(z)).astype(h_ref.dtype)
            return carry

        lax.fori_loop(0, tm // ROW_CHUNK, ln_body, 0)

    y = jnp.dot(h_ref[...], pw_ref[...], preferred_element_type=F32) + pwb_ref[...]
    o_ref[...] = x_ref[...] + gate_ref[0] * y


def _conv_pw(a, x, mod, rows, seq_len, dw_w, dw_b, ln_g, ln_b, pw_bf, pw_b):
    t, d = a.shape
    row_of = rows.row_of
    tm = _tile(seq_len, 512)
    tn = _tile(d, 512)
    nj = d // tn
    tiles_per_seq = seq_len // tm
    hb = tm // HALO
    n_hb = t // HALO
    return pl.pallas_call(
        functools.partial(_conv_pw_kernel, tiles_per_seq),
        grid=(t // tm, nj),
        in_specs=[pl.BlockSpec((HALO, d), lambda i, j: (jnp.maximum(i * hb - 1, 0), 0)),
                  pl.BlockSpec((tm, d), lambda i, j: (i, 0)),
                  pl.BlockSpec((HALO, d), lambda i, j: (jnp.minimum((i + 1) * hb, n_hb - 1), 0)),
                  pl.BlockSpec((CONV_WIDTH * V7X_SUBLANES, d), lambda i, j: (0, 0)),
                  pl.BlockSpec((1, d), lambda i, j: (0, 0)),
                  pl.BlockSpec((1, d), lambda i, j: (0, 0)),
                  pl.BlockSpec((1, d), lambda i, j: (0, 0)),
                  pl.BlockSpec((d, tn), lambda i, j: (0, j)),
                  pl.BlockSpec((1, tn), lambda i, j: (0, j)),
                  pl.BlockSpec((tm, tn), lambda i, j: (i, j)),
                  pl.BlockSpec((1, 1, tn), lambda i, j: (row_of(i * tm), 0, 2 * nj + j))],
        out_specs=pl.BlockSpec((tm, tn), lambda i, j: (i, j)),
        out_shape=jax.ShapeDtypeStruct((t, d), F32),
        scratch_shapes=[pltpu.VMEM((tm + 2 * HALO, d), F32),
                        pltpu.VMEM((tm, d), F32),
                        pltpu.VMEM((tm, d), BF16),
                        pltpu.VMEM((V7X_SUBLANES, tm + 2 * HALO, V7X_LANES), F32)],
        compiler_params=_params(("parallel", "arbitrary")),
        name="conv_ln_pw",
    )(a, a, a, jnp.repeat(dw_w, V7X_SUBLANES, axis=0), dw_b.reshape(1, d), ln_g.reshape(1, d), ln_b.reshape(1, d),
      pw_bf, pw_b.reshape(1, d), x, mod)


def _merge_exchange_pairs(n):
    pairs = []
    p = 1
    while p < n:
        k = p
        while k >= 1:
            for j in range(k % p, n - k, 2 * k):
                for i in range(min(k, n - j - k)):
                    if (i + j) // (2 * p) == (i + j + k) // (2 * p):
                        pairs.append((i + j, i + j + k))
            k //= 2
        p *= 2
    return pairs


def _top_values(s, val_ref, cnt_ref):
    groups = s.shape[0] // V7X_SUBLANES
    assert groups == PEER_TOPK
    lists = [s[g * V7X_SUBLANES:(g + 1) * V7X_SUBLANES, :] for g in range(groups)]
    for i, j in _merge_exchange_pairs(groups):
        hi, lo = jnp.maximum(lists[i], lists[j]), jnp.minimum(lists[i], lists[j])
        lists[i], lists[j] = hi, lo
    for it in range(PEER_TOPK):
        m = jnp.max(lists[0], axis=0, keepdims=True)
        eq = lists[0] == m
        val_ref[it:it + 1, :] = m
        cnt_ref[it:it + 1, :] = jnp.sum(jnp.where(eq, 1.0, 0.0), axis=0, keepdims=True)
        for g in range(PEER_TOPK - it - 1):
            lists[g] = jnp.where(eq, lists[g + 1], lists[g])


def _candidate_blocks(a_ref, b_ref, combine):
    half = PEER_TOPK // 2
    blocks = [combine(a_ref[0:1, :], b_ref[...]),
              combine(a_ref[half:PEER_TOPK, :], b_ref[0:1, :])]
    for a in range(1, half):
        blocks.append(combine(a_ref[a:a + 1, :], b_ref[0:half, :]))
    return jnp.concatenate(blocks, axis=0)


def _pair_threshold(s1, s2, v1_ref, n1_ref, v2_ref, n2_ref):
    n = s1.shape[1]
    _top_values(s1, v1_ref, n1_ref)
    _top_values(s2, v2_ref, n2_ref)
    cand = _candidate_blocks(v1_ref, v2_ref, lambda a, b: a + b)
    wgt = _candidate_blocks(n1_ref, n2_ref, lambda a, b: a * b)
    work = cand
    cnt = jnp.zeros((1, n), F32)
    thr = jnp.zeros((1, n), F32)
    for _ in range(PEER_TOPK):
        m = jnp.max(work, axis=0, keepdims=True)
        eq = work == m
        thr = jnp.where(cnt < PEER_TOPK, m, thr)
        cnt = cnt + jnp.sum(jnp.where(eq, wgt, 0.0), axis=0, keepdims=True)
        work = jnp.where(eq, -jnp.inf, work)
    m1 = v1_ref[0:1, :]
    m2 = v2_ref[0:1, :]
    z = jnp.sum(jnp.where(cand >= thr, wgt * jnp.exp(cand - (m1 + m2)), 0.0), axis=0, keepdims=True)
    return thr, m1, m2, z


def _router_kernel(x_ref, g_ref, sh_ref, sc_ref, wq_ref, k1_ref, k2_ref,
                   ht_ref, s1_ref, s2_ref, aux_ref,
                   h_ref, v1_ref, n1_ref, v2_ref, n2_ref):
    tt, d = x_ref.shape

    @pl.when(pl.program_id(1) == 0)
    def _():
        _norm_mod_into(x_ref, g_ref, sh_ref, sc_ref, h_ref)

        def tr_body(c, carry):
            cb = pl.multiple_of(c * V7X_LANES, V7X_LANES)
            blk = h_ref[:, pl.ds(cb, V7X_LANES)].astype(F32)
            ht_ref[pl.ds(cb, V7X_LANES), :] = blk.T.astype(ht_ref.dtype)
            return carry

        lax.fori_loop(0, d // V7X_LANES, tr_body, 0)

    q = jnp.dot(h_ref[...], wq_ref[...], preferred_element_type=F32).astype(BF16)
    nt = (((1,), (1,)), ((), ()))
    s1 = lax.dot_general(k1_ref[...], q[:, :PEER_HALF], nt, preferred_element_type=F32)
    s2 = lax.dot_general(k2_ref[...], q[:, PEER_HALF:], nt, preferred_element_type=F32)
    s1_ref[0] = s1
    s2_ref[0] = s2

    thr, m1, m2, z = _pair_threshold(s1, s2, v1_ref, n1_ref, v2_ref, n2_ref)
    aux_ref[0, 0:1, :] = thr
    aux_ref[0, 1:2, :] = m2
    aux_ref[0, 2:3, :] = m1 + jnp.log(z)
    aux_ref[0, 3:V7X_SUBLANES, :] = jnp.zeros((V7X_SUBLANES - 3, tt), F32)


def _router(x, mod, rows, norm_g, wq_bf, k1_bf, k2_bf):
    t, d = x.shape
    tt = _tile(rows.span, 512)
    row_of = rows.row_of
    nh = PEER_HEADS
    nk = PEER_NKEYS
    return pl.pallas_call(
        _router_kernel,
        grid=(t // tt, nh),
        in_specs=[pl.BlockSpec((tt, d), lambda i, j: (i, 0)),
                  pl.BlockSpec((1, d), lambda i, j: (0, 0)),
                  _mod_spec(d, 3, lambda i: row_of(i * tt)), _mod_spec(d, 4, lambda i: row_of(i * tt)),
                  pl.BlockSpec((d, PEER_QDIM), lambda i, j: (0, j)),
                  pl.BlockSpec((nk, PEER_HALF), lambda i, j: (0, 0)),
                  pl.BlockSpec((nk, PEER_HALF), lambda i, j: (0, 0))],
        out_specs=[pl.BlockSpec((d, tt), lambda i, j: (0, i)),
                   pl.BlockSpec((1, nk, tt), lambda i, j: (j, 0, i)),
                   pl.BlockSpec((1, nk, tt), lambda i, j: (j, 0, i)),
                   pl.BlockSpec((1, V7X_SUBLANES, tt), lambda i, j: (j, 0, i))],
        out_shape=[jax.ShapeDtypeStruct((d, t), BF16),
                   jax.ShapeDtypeStruct((nh, nk, t), F32),
                   jax.ShapeDtypeStruct((nh, nk, t), F32),
                   jax.ShapeDtypeStruct((nh, V7X_SUBLANES, t), F32)],
        scratch_shapes=[pltpu.VMEM((tt, d), BF16)] + [pltpu.VMEM((PEER_TOPK, tt), F32)] * 4,
        compiler_params=_params(("parallel", "arbitrary")),
        name="peer_router",
    )(x, norm_g.reshape(1, d), mod, mod, wq_bf, k1_bf, k2_bf)


def _half_gate_rows(n1, s1_ref, s2_ref, e2_ref, aux_ref):
    w = None
    for h in range(s1_ref.shape[0]):
        s1row = s1_ref[h, pl.ds(n1, 1), :]
        g1row = 0.5 * jnp.exp(s1row - aux_ref[h, 2:3, :])
        sel = (s1row + s2_ref[h]) >= aux_ref[h, 0:1, :]
        term = jnp.where(sel, g1row * e2_ref[h], 0.0)
        w = term if w is None else w + term
    return w


def _expert_kernel(ht_ref, u_ref, v_ref, s1_ref, s2_ref, aux_ref, o_ref, e2_ref):
    te = u_ref.shape[0]
    nh = s1_ref.shape[0]
    e = pl.program_id(1)
    blocks = te // PEER_NKEYS

    @pl.when(e == 0)
    def _():
        o_ref[...] = jnp.zeros(o_ref.shape, o_ref.dtype)
        for h in range(nh):
            e2_ref[h] = jnp.exp(s2_ref[h] - aux_ref[h, 1:2, :])

    x = jnp.dot(u_ref[...], ht_ref[...], preferred_element_type=F32)
    c = math.sqrt(2.0 / math.pi)
    t = jnp.tanh(x * (c + (c * 0.044715) * (x * x)))
    half_w = jnp.concatenate(
        [_half_gate_rows(e * blocks + r, s1_ref, s2_ref, e2_ref, aux_ref) for r in range(blocks)], axis=0)
    p = ((x + x * t) * half_w).astype(BF16)
    tn = (((0,), (0,)), ((), ()))
    o_ref[...] += lax.dot_general(p, v_ref[...], tn, preferred_element_type=F32)


def _experts(ht, s1, s2, aux, u_bf, v_bf):
    d, t = ht.shape
    ne = u_bf.shape[0]
    nh, nk, _ = s1.shape
    tt = _tile(t, 512)
    te = _tile(ne, 512)
    return pl.pallas_call(
        _expert_kernel,
        grid=(t // tt, ne // te),
        in_specs=[pl.BlockSpec((d, tt), lambda i, e: (0, i)),
                  pl.BlockSpec((te, d), lambda i, e: (e, 0)),
                  pl.BlockSpec((te, d), lambda i, e: (e, 0)),
                  pl.BlockSpec((nh, nk, tt), lambda i, e: (0, 0, i)),
                  pl.BlockSpec((nh, nk, tt), lambda i, e: (0, 0, i)),
                  pl.BlockSpec((nh, V7X_SUBLANES, tt), lambda i, e: (0, 0, i))],
        out_specs=pl.BlockSpec((tt, d), lambda i, e: (i, 0)),
        out_shape=jax.ShapeDtypeStruct((t, d), F32),
        scratch_shapes=[pltpu.VMEM((nh, nk, tt), F32)],
        compiler_params=_params(("parallel", "arbitrary")),
        name="peer_experts",
    )(ht, u_bf, v_bf, s1, s2, aux)


def _residual_kernel(x_ref, f_ref, gate_ref, o_ref):
    o_ref[...] = x_ref[...] + gate_ref[0] * f_ref[...]


def _residual(x, f, mod, rows, chunk):
    t, d = x.shape
    tm = _tile(rows.span, 256)
    row_of = rows.row_of
    return pl.pallas_call(
        _residual_kernel,
        grid=(t // tm,),
        in_specs=[pl.BlockSpec((tm, d), lambda i: (i, 0)),
                  pl.BlockSpec((tm, d), lambda i: (i, 0)),
                  pl.BlockSpec((1, 1, d), lambda i: (row_of(i * tm), 0, chunk))],
        out_specs=pl.BlockSpec((tm, d), lambda i: (i, 0)),
        out_shape=jax.ShapeDtypeStruct((t, d), F32),
        compiler_params=_params(("parallel",)),
        name="gated_residual",
    )(x, f, mod)


def _peer_layer(x, mod, rows, norm_g, wq_bf, k1_bf, k2_bf, u_bf, v_bf):
    ht, s1, s2, aux = _router(x, mod, rows, norm_g, wq_bf, k1_bf, k2_bf)
    f = _experts(ht, s1, s2, aux, u_bf, v_bf)
    return _residual(x, f, mod, rows, 5)


def _norm_mod_kernel(x_ref, g_ref, sh_ref, sc_ref, h_ref):
    _norm_mod_into(x_ref, g_ref, sh_ref, sc_ref, h_ref)


def _norm_mod(x, mod, rows, norm_g):
    t, d = x.shape
    tm = _tile(rows.span, 256)
    row_of = rows.row_of
    spec = lambda chunk: pl.BlockSpec((1, 1, d), lambda i: (row_of(i * tm), 0, chunk))
    return pl.pallas_call(
        _norm_mod_kernel,
        grid=(t // tm,),
        in_specs=[pl.BlockSpec((tm, d), lambda i: (i, 0)),
                  pl.BlockSpec((1, d), lambda i: (0, 0)),
                  spec(0), spec(1)],
        out_specs=pl.BlockSpec((tm, d), lambda i: (i, 0)),
        out_shape=jax.ShapeDtypeStruct((t, d), BF16),
        compiler_params=_params(("parallel",)),
        name="norm_modulate",
    )(x, norm_g.reshape(1, d), mod, mod)


def _qkv_kernel(rotary, n_qk, h_ref, w_ref, qg_ref, kg_ref, cos_ref, sa_ref, sb_ref, o_ref):
    j = pl.program_id(1)
    acc = jnp.dot(h_ref[...], w_ref[...], preferred_element_type=F32)
    tn = acc.shape[1]

    def qk_norm(gain_ref, scale):
        cols = []
        for c in range(tn // DA_HEAD_DIM):
            t = acc[:, c * DA_HEAD_DIM:(c + 1) * DA_HEAD_DIM]
            t = t * lax.rsqrt(jnp.mean(t * t, axis=-1, keepdims=True) + EPS) * gain_ref[...]
            if rotary:
                t = (t * cos_ref[...]
                     + pltpu.roll(t, DA_HEAD_DIM - ROPE_FREQS, axis=1) * sa_ref[...]
                     + pltpu.roll(t, ROPE_FREQS, axis=1) * sb_ref[...])
            cols.append(t * scale if scale != 1.0 else t)
        o_ref[...] = jnp.concatenate(cols, axis=1).astype(o_ref.dtype)

    @pl.when(j < n_qk)
    def _():
        qk_norm(qg_ref, DA_SCALE * LOG2E)

    @pl.when(jnp.logical_and(j >= n_qk, j < 2 * n_qk))
    def _():
        qk_norm(kg_ref, 1.0)

    @pl.when(j >= 2 * n_qk)
    def _():
        o_ref[...] = acc.astype(o_ref.dtype)


def _rotary_tables(seq_len):
    t = jnp.arange(seq_len, dtype=jnp.int32)
    row = (t // GRID_W).astype(F32)
    col = (t % GRID_W).astype(F32)
    inv = ROPE_BASE ** (-jnp.arange(ROPE_FREQS, dtype=F32) / ROPE_FREQS)
    ar = row[:, None] * inv
    ac = col[:, None] * inv
    ang = jnp.concatenate([ar, ar, ac, ac], axis=-1)
    cos, sin = jnp.cos(ang), jnp.sin(ang)
    first_half = (jnp.arange(DA_HEAD_DIM) % (2 * ROPE_FREQS)) < ROPE_FREQS
    sin_a = jnp.where(first_half, -sin, 0.0)
    sin_b = jnp.where(first_half, 0.0, sin)
    return cos, sin_a, sin_b


def _qkv(x, mod, rows, seq_len, norm_g, w_bf, q_gain, k_gain, rotary):
    t, d = x.shape
    h = _norm_mod(x, mod, rows, norm_g)
    tm = _tile(seq_len, 1024)
    tn = _tile(d, 512)
    n_qk = d // tn
    tiles_per_seq = seq_len // tm
    cos, sin_a, sin_b = _rotary_tables(seq_len)
    tab_spec = pl.BlockSpec((tm, DA_HEAD_DIM), lambda i, j: (i % tiles_per_seq, 0))
    return pl.pallas_call(
        functools.partial(_qkv_kernel, rotary, n_qk),
        grid=(t // tm, 3 * n_qk),
        in_specs=[pl.BlockSpec((tm, d), lambda i, j: (i, 0)),
                  pl.BlockSpec((d, tn), lambda i, j: (0, j)),
                  pl.BlockSpec((1, DA_HEAD_DIM), lambda i, j: (0, 0)),
                  pl.BlockSpec((1, DA_HEAD_DIM), lambda i, j: (0, 0)),
                  tab_spec, tab_spec, tab_spec],
        out_specs=pl.BlockSpec((tm, tn), lambda i, j: (i, j)),
        out_shape=jax.ShapeDtypeStruct((t, 3 * d), BF16),
        compiler_params=_params(("parallel", "arbitrary")),
        name="qkv_proj",
    )(h, w_bf, q_gain.reshape(1, -1), k_gain.reshape(1, -1), cos, sin_a, sin_b)


def _attn_kernel(lam_init, q_ref, kl_ref, vl_ref, kc_ref, vc_ref, lam_ref, sg_ref, o_ref):
    hd = DA_HEAD_DIM
    lq1, lk1, lq2, lk2 = (lam_ref[r:r + 1, :] for r in range(4))
    lam = (jnp.exp(jnp.sum(lq1 * lk1, axis=-1, keepdims=True))
           - jnp.exp(jnp.sum(lq2 * lk2, axis=-1, keepdims=True)) + lam_init)
    nt = (((1,), (1,)), ((), ()))
    v_lat = vl_ref[...]
    v_ctx = vc_ref[...]

    def attend(sub):
        q = q_ref[:, sub * hd:(sub + 1) * hd]
        s_lat = lax.dot_general(q, kl_ref[:, sub * hd:(sub + 1) * hd], nt, preferred_element_type=F32)
        s_ctx = lax.dot_general(q, kc_ref[:, sub * hd:(sub + 1) * hd], nt, preferred_element_type=F32)
        m = jnp.maximum(jnp.max(s_lat, axis=-1, keepdims=True), jnp.max(s_ctx, axis=-1, keepdims=True))
        p_lat = jnp.exp2(s_lat - m)
        p_ctx = jnp.exp2(s_ctx - m)
        z = jnp.sum(p_lat, axis=-1, keepdims=True) + jnp.sum(p_ctx, axis=-1, keepdims=True)
        o = (jnp.dot(p_lat.astype(BF16), v_lat, preferred_element_type=F32)
             + jnp.dot(p_ctx.astype(BF16), v_ctx, preferred_element_type=F32))
        return o / z

    o = attend(0) - lam * attend(1)
    o = o * lax.rsqrt(jnp.mean(o * o, axis=-1, keepdims=True) + EPS) * sg_ref[...]
    o_ref[...] = (o * (1.0 - lam_init)).astype(o_ref.dtype)


def _attention(qkv_lat, qkv_ctx, batch, lam_rows, subln_g, lam_init):
    t, d3 = qkv_lat.shape
    d = d3 // 3
    nh = d // DA_VDIM
    seq = t // batch
    ctx_len = qkv_ctx.shape[0] // batch
    tq = _tile(seq, 512)
    nq = seq // tq
    return pl.pallas_call(
        functools.partial(_attn_kernel, lam_init),
        grid=(batch, nh, nq),
        in_specs=[pl.BlockSpec((tq, DA_VDIM), lambda b, h, i: (b * nq + i, h)),
                  pl.BlockSpec((seq, DA_VDIM), lambda b, h, i: (b, nh + h)),
                  pl.BlockSpec((seq, DA_VDIM), lambda b, h, i: (b, 2 * nh + h)),
                  pl.BlockSpec((ctx_len, DA_VDIM), lambda b, h, i: (b, nh + h)),
                  pl.BlockSpec((ctx_len, DA_VDIM), lambda b, h, i: (b, 2 * nh + h)),
                  pl.BlockSpec((4, DA_HEAD_DIM), lambda b, h, i: (0, 0)),
                  pl.BlockSpec((1, DA_VDIM), lambda b, h, i: (0, 0))],
        out_specs=pl.BlockSpec((tq, DA_VDIM), lambda b, h, i: (b * nq + i, h)),
        out_shape=jax.ShapeDtypeStruct((t, d), BF16),
        compiler_params=_params(("parallel", "parallel", "arbitrary")),
        name="diff_attention",
    )(qkv_lat, qkv_lat, qkv_lat, qkv_ctx, qkv_ctx, lam_rows, subln_g.reshape(1, DA_VDIM))


def _oproj_kernel(a_ref, w_ref, x_ref, gate_ref, o_ref):
    y = jnp.dot(a_ref[...], w_ref[...], preferred_element_type=F32)
    o_ref[...] = x_ref[...] + gate_ref[0] * y


def _oproj(a, w_bf, x, mod, rows):
    t, d = x.shape
    tm = _tile(rows.span, 1024)
    row_of = rows.row_of
    tn = _tile(d, 512)
    nj = d // tn
    return pl.pallas_call(
        _oproj_kernel,
        grid=(t // tm, nj),
        in_specs=[pl.BlockSpec((tm, d), lambda i, j: (i, 0)),
                  pl.BlockSpec((d, tn), lambda i, j: (0, j)),
                  pl.BlockSpec((tm, tn), lambda i, j: (i, j)),
                  pl.BlockSpec((1, 1, tn), lambda i, j: (row_of(i * tm), 0, 2 * nj + j))],
        out_specs=pl.BlockSpec((tm, tn), lambda i, j: (i, j)),
        out_shape=jax.ShapeDtypeStruct((t, d), F32),
        compiler_params=_params(("parallel", "arbitrary")),
        name="attn_out_proj",
    )(a, w_bf, x, mod)


def kernel(x, c, ctx, c_ctx, l0_ada_w, l0_ada_b, l0_norm1_g, l0_glu_w, l0_glu_b, l0_dw_w, l0_dw_b, l0_ln_g, l0_ln_b, l0_pw_w, l0_pw_b, l0_norm2_g, l0_peer_wq, l0_peer_k1, l0_peer_k2, l0_peer_u, l0_peer_v, l1_ada_w, l1_ada_b, l1_norm1_g, l1_qkv_w, l1_q_norm_g, l1_k_norm_g, l1_lam_q1, l1_lam_k1, l1_lam_q2, l1_lam_k2, l1_subln_g, l1_o_w, l1_norm2_g, l1_peer_wq, l1_peer_k1, l1_peer_k2, l1_peer_u, l1_peer_v):
    batch, seq, d = x.shape
    ctx_len = ctx.shape[1]
    assert seq % GRID_W == 0 and d % DA_VDIM == 0 and batch < 8
    xl = x.reshape(batch * seq, d)
    xc = ctx.reshape(batch * ctx_len, d)
    ctx_row = batch
    cvec = jnp.concatenate([c, c_ctx[None, :], jnp.zeros((8 - batch - 1, d), F32)], axis=0)

    lat_row = ModRows(lambda tok: tok // seq, seq)
    ctx_row_of = ModRows(lambda tok: ctx_row, batch * ctx_len)

    bf = lambda w: w.astype(BF16)

    mod0 = _adaln(cvec, l0_ada_w, l0_ada_b)
    glu_bf, pw_bf = bf(l0_glu_w), bf(l0_pw_w)
    peer0 = (bf(l0_peer_wq), bf(l0_peer_k1), bf(l0_peer_k2), bf(l0_peer_u), bf(l0_peer_v))
    conv_args = (l0_dw_w, l0_dw_b, l0_ln_g, l0_ln_b, pw_bf, l0_pw_b)

    def layer0(xs, row_of, seq_len):
        a = _glu(xs, mod0, row_of, l0_norm1_g, glu_bf, l0_glu_b)
        x1 = _conv_pw(a, xs, mod0, row_of, seq_len, *conv_args)
        return _peer_layer(x1, mod0, row_of, l0_norm2_g, *peer0)

    xl = layer0(xl, lat_row, seq)
    xc = layer0(xc, ctx_row_of, ctx_len)

    mod1 = _adaln(cvec, l1_ada_w, l1_ada_b)
    lam_init = 0.8 - 0.6 * math.exp(-0.3 * 1)
    qkv_bf = bf(l1_qkv_w)
    qkv_lat = _qkv(xl, mod1, lat_row, seq, l1_norm1_g, qkv_bf, l1_q_norm_g, l1_k_norm_g, True)
    qkv_ctx = _qkv(xc, mod1, ctx_row_of, ctx_len, l1_norm1_g, qkv_bf, l1_q_norm_g, l1_k_norm_g, False)
    lam_rows = jnp.stack([l1_lam_q1, l1_lam_k1, l1_lam_q2, l1_lam_k2], axis=0)
    o = _attention(qkv_lat, qkv_ctx, batch, lam_rows, l1_subln_g, lam_init)
    xl = _oproj(o, bf(l1_o_w), xl, mod1, lat_row)
    peer1 = (bf(l1_peer_wq), bf(l1_peer_k1), bf(l1_peer_k2), bf(l1_peer_u), bf(l1_peer_v))
    xl = _peer_layer(xl, mod1, lat_row, l1_norm2_g, *peer1)
    return xl.reshape(batch, seq, d)
```

```python
import functools
import math
from typing import Callable, NamedTuple

import jax
import jax.numpy as jnp
from jax import lax
from jax.experimental import pallas as pl
from jax.experimental.pallas import tpu as pltpu

F32 = jnp.float32
BF16 = jnp.bfloat16

GRID_W = 64
CONV_WIDTH = 31
CONV_PAD = (CONV_WIDTH - 1) // 2
DA_HEAD_DIM = 128
DA_VDIM = 2 * DA_HEAD_DIM
DA_SCALE = DA_HEAD_DIM ** -0.5
ROPE_BASE = 10000.0
ROPE_FREQS = DA_HEAD_DIM // 4
PEER_HEADS = 8
PEER_NKEYS = 128
PEER_QDIM = 256
PEER_HALF = PEER_QDIM // 2
PEER_TOPK = 16
EPS = 1e-6
LOG2E = math.log2(math.e)

V7X_LANES = 128
V7X_SUBLANES = 8
V7X_BF16_ROWS = 16
V7X_VMEM_BYTES = 64 * 1024 * 1024
VMEM_LIMIT = V7X_VMEM_BYTES - 8 * 1024 * 1024

ROW_CHUNK = 64
COL_CHUNK = 512
HALO = 16


def _tile(n, pref):
    t = min(n, pref)
    while n % t:
        t //= 2
    return t


def _params(sem):
    return pltpu.CompilerParams(dimension_semantics=sem, vmem_limit_bytes=VMEM_LIMIT)


def _ada_kernel(c_ref, w_ref, b_ref, o_ref):
    c = c_ref[...]
    s = (c * jax.nn.sigmoid(c)).astype(BF16)
    o_ref[...] = jnp.dot(s, w_ref[...].astype(BF16), preferred_element_type=F32) + b_ref[...]


def _adaln(cvec, w, b):
    d, n = w.shape
    tn = _tile(n, 512)
    out = pl.pallas_call(
        _ada_kernel,
        grid=(n // tn,),
        in_specs=[pl.BlockSpec((8, d), lambda j: (0, 0)),
                  pl.BlockSpec((d, tn), lambda j: (0, j)),
                  pl.BlockSpec((1, tn), lambda j: (0, j))],
        out_specs=pl.BlockSpec((8, tn), lambda j: (0, j)),
        out_shape=jax.ShapeDtypeStruct((8, n), F32),
        compiler_params=_params(("arbitrary",)),
        name="adaln",
    )(cvec, w, b.reshape(1, n))
    return out.reshape(8, 1, n)


def _norm_mod_into(x_ref, g_ref, sh_ref, sc_ref, h_ref):
    rows, d = x_ref.shape
    sub_rows = V7X_BF16_ROWS
    cc = min(d, COL_CHUNK)

    def body(r, carry):
        for s in range(ROW_CHUNK // sub_rows):
            sl = pl.ds(pl.multiple_of(r * ROW_CHUNK + s * sub_rows, sub_rows), sub_rows)
            part = None
            for c in range(d // cc):
                xs = x_ref[sl, c * cc:(c + 1) * cc]
                sq = xs * xs
                for lb in range(cc // V7X_LANES):
                    piece = sq[:, lb * V7X_LANES:(lb + 1) * V7X_LANES]
                    part = piece if part is None else part + piece
            inv = lax.rsqrt(jnp.sum(part, axis=-1, keepdims=True) * (1.0 / d) + EPS)
            for c in range(d // cc):
                cols = slice(c * cc, (c + 1) * cc)
                y = x_ref[sl, cols] * inv
                h_ref[sl, cols] = ((y * g_ref[:, cols]) * (1.0 + sc_ref[0, :, cols]) + sh_ref[0, :, cols]
                                   ).astype(h_ref.dtype)
        return carry

    lax.fori_loop(0, rows // ROW_CHUNK, body, 0)


class ModRows(NamedTuple):
    row_of: Callable
    span: int


def _mod_spec(d, chunk, row_of):
    return pl.BlockSpec((1, 1, d), lambda i, j: (row_of(i), 0, chunk))


def _glu_kernel(x_ref, g_ref, sh_ref, sc_ref, wl_ref, wr_ref, bl_ref, br_ref, o_ref, h_ref):
    @pl.when(pl.program_id(1) == 0)
    def _():
        _norm_mod_into(x_ref, g_ref, sh_ref, sc_ref, h_ref)

    h = h_ref[...]
    left = jnp.dot(h, wl_ref[...], preferred_element_type=F32) + bl_ref[...]
    right = jnp.dot(h, wr_ref[...], preferred_element_type=F32) + br_ref[...]
    o_ref[...] = (left * jax.nn.sigmoid(right)).astype(o_ref.dtype)


def _glu(x, mod, rows, norm_g, w_bf, b):
    t, d = x.shape
    tm = _tile(rows.span, 512)
    row_of = rows.row_of
    tn = _tile(d, 512)
    nj = d // tn
    return pl.pallas_call(
        _glu_kernel,
        grid=(t // tm, nj),
        in_specs=[pl.BlockSpec((tm, d), lambda i, j: (i, 0)),
                  pl.BlockSpec((1, d), lambda i, j: (0, 0)),
                  _mod_spec(d, 0, lambda i: row_of(i * tm)), _mod_spec(d, 1, lambda i: row_of(i * tm)),
                  pl.BlockSpec((d, tn), lambda i, j: (0, j)),
                  pl.BlockSpec((d, tn), lambda i, j: (0, j + nj)),
                  pl.BlockSpec((1, tn), lambda i, j: (0, j)),
                  pl.BlockSpec((1, tn), lambda i, j: (0, j + nj))],
        out_specs=pl.BlockSpec((tm, tn), lambda i, j: (i, j)),
        out_shape=jax.ShapeDtypeStruct((t, d), BF16),
        scratch_shapes=[pltpu.VMEM((tm, d), BF16)],
        compiler_params=_params(("parallel", "arbitrary")),
        name="glu_proj",
    )(x, norm_g.reshape(1, d), mod, mod, w_bf, w_bf, b.reshape(1, 2 * d), b.reshape(1, 2 * d))


def _conv_pw_kernel(tiles_per_seq, ap_ref, am_ref, an_ref, dw_ref, dwb_ref, lng_ref, lnb_ref,
                    pw_ref, pwb_ref, x_ref, gate_ref, o_ref, abuf_ref, y_ref, h_ref, sh_ref):
    tm, d = am_ref.shape

    @pl.when(pl.program_id(1) == 0)
    def _():
        pos = pl.program_id(0) % tiles_per_seq
        prev = ap_ref[...].astype(F32)
        nxt = an_ref[...].astype(F32)
        abuf_ref[0:HALO, :] = jnp.where(pos == 0, 0.0, prev)
        abuf_ref[HALO:HALO + tm, :] = am_ref[...].astype(F32)
        abuf_ref[HALO + tm:HALO + tm + HALO, :] = jnp.where(pos == tiles_per_seq - 1, 0.0, nxt)

        first = HALO - CONV_PAD
        sub = V7X_SUBLANES

        def conv_body(c, carry):
            lanes = pl.ds(pl.multiple_of(c * V7X_LANES, V7X_LANES), V7X_LANES)
            col = abuf_ref[:, lanes]
            n_keep = tm + 2 * HALO - sub
            for r in range(sub):
                sh_ref[r, 0:n_keep, :] = col[r:r + n_keep, :]
            taps = [dw_ref[k * sub:(k + 1) * sub, lanes] for k in range(CONV_WIDTH)]
            bias = jnp.broadcast_to(dwb_ref[:, lanes], (sub, V7X_LANES))
            for g in range(tm // sub):
                acc = bias
                for k in range(CONV_WIDTH):
                    q, r = divmod(first + k, sub)
                    acc = acc + taps[k] * sh_ref[r, (g + q) * sub:(g + q + 1) * sub, :]
                y_ref[g * sub:(g + 1) * sub, lanes] = acc
            return carry

        lax.fori_loop(0, d // V7X_LANES, conv_body, 0)

        sub_rows = V7X_BF16_ROWS
        cc = min(d, COL_CHUNK)

        def row_sum(sl, fn):
            part = None
            for c in range(d // cc):
                blk = fn(y_ref[sl, c * cc:(c + 1) * cc])
                for lb in range(cc // V7X_LANES):
                    piece = blk[:, lb * V7X_LANES:(lb + 1) * V7X_LANES]
                    part = piece if part is None else part + piece
            return jnp.sum(part, axis=-1, keepdims=True)

        def ln_body(r, carry):
            for s in range(ROW_CHUNK // sub_rows):
                sl = pl.ds(pl.multiple_of(r * ROW_CHUNK + s * sub_rows, sub_rows), sub_rows)
                mu = row_sum(sl, lambda b: b) * (1.0 / d)
                inv = lax.rsqrt(row_sum(sl, lambda b: (b - mu) * (b - mu)) * (1.0 / d) + EPS)
                for c in range(d // cc):
                    cols = slice(c * cc, (c + 1) * cc)
                    z = ((y_ref[sl, cols] - mu) * inv) * lng_ref[:, cols] + lnb_ref[:, cols]
                    h_ref[sl, cols] = (z * jax.nn.sigmoid(z)).astype(h_ref.dtype)
            return carry

        lax.fori_loop(0, tm // ROW_CHUNK, ln_body, 0)

    y = jnp.dot(h_ref[...], pw_ref[...], preferred_element_type=F32) + pwb_ref[...]
    o_ref[...] = x_ref[...] + gate_ref[0] * y


def _conv_pw(a, x, mod, rows, seq_len, dw_w, dw_b, ln_g, ln_b, pw_bf, pw_b):
    t, d = a.shape
    row_of = rows.row_of
    tm = _tile(seq_len, 512)
    tn = _tile(d, 512)
    nj = d // tn
    tiles_per_seq = seq_len // tm
    hb = tm // HALO
    n_hb = t // HALO
    return pl.pallas_call(
        functools.partial(_conv_pw_kernel, tiles_per_seq),
        grid=(t // tm, nj),
        in_specs=[pl.BlockSpec((HALO, d), lambda i, j: (jnp.maximum(i * hb - 1, 0), 0)),
                  pl.BlockSpec((tm, d), lambda i, j: (i, 0)),
                  pl.BlockSpec((HALO, d), lambda i, j: (jnp.minimum((i + 1) * hb, n_hb - 1), 0)),
                  pl.BlockSpec((CONV_WIDTH * V7X_SUBLANES, d), lambda i, j: (0, 0)),
                  pl.BlockSpec((1, d), lambda i, j: (0, 0)),
                  pl.BlockSpec((1, d), lambda i, j: (0, 0)),
                  pl.BlockSpec((1, d), lambda i, j: (0, 0)),
                  pl.BlockSpec((d, tn), lambda i, j: (0, j)),
                  pl.BlockSpec((1, tn), lambda i, j: (0, j)),
                  pl.BlockSpec((tm, tn), lambda i, j: (i, j)),
                  pl.BlockSpec((1, 1, tn), lambda i, j: (row_of(i * tm), 0, 2 * nj + j))],
        out_specs=pl.BlockSpec((tm, tn), lambda i, j: (i, j)),
        out_shape=jax.ShapeDtypeStruct((t, d), F32),
        scratch_shapes=[pltpu.VMEM((tm + 2 * HALO, d), F32),
                        pltpu.VMEM((tm, d), F32),
                        pltpu.VMEM((tm, d), BF16),
                        pltpu.VMEM((V7X_SUBLANES, tm + 2 * HALO, V7X_LANES), F32)],
        compiler_params=_params(("parallel", "arbitrary")),
        name="conv_ln_pw",
    )(a, a, a, jnp.repeat(dw_w, V7X_SUBLANES, axis=0), dw_b.reshape(1, d), ln_g.reshape(1, d), ln_b.reshape(1, d),
      pw_bf, pw_b.reshape(1, d), x, mod)


def _merge_exchange_pairs(n):
    pairs = []
    p = 1
    while p < n:
        k = p
        while k >= 1:
            for j in range(k % p, n - k, 2 * k):
                for i in range(min(k, n - j - k)):
                    if (i + j) // (2 * p) == (i + j + k) // (2 * p):
                        pairs.append((i + j, i + j + k))
            k //= 2
        p *= 2
    return pairs


def _top_values(s, val_ref, cnt_ref):
    groups = s.shape[0] // V7X_SUBLANES
    assert groups == PEER_TOPK
    lists = [s[g * V7X_SUBLANES:(g + 1) * V7X_SUBLANES, :] for g in range(groups)]
    for i, j in _merge_exchange_pairs(groups):
        hi, lo = jnp.maximum(lists[i], lists[j]), jnp.minimum(lists[i], lists[j])
        lists[i], lists[j] = hi, lo
    for it in range(PEER_TOPK):
        m = jnp.max(lists[0], axis=0, keepdims=True)
        eq = lists[0] == m
        val_ref[it:it + 1, :] = m
        cnt_ref[it:it + 1, :] = jnp.sum(jnp.where(eq, 1.0, 0.0), axis=0, keepdims=True)
        for g in range(PEER_TOPK - it - 1):
            lists[g] = jnp.where(eq, lists[g + 1], lists[g])


def _candidate_blocks(a_ref, b_ref, combine):
    half = PEER_TOPK // 2
    blocks = [combine(a_ref[0:1, :], b_ref[...]),
              combine(a_ref[half:PEER_TOPK, :], b_ref[0:1, :])]
    for a in range(1, half):
        blocks.append(combine(a_ref[a:a + 1, :], b_ref[0:half, :]))
    return jnp.concatenate(blocks, axis=0)


def _pair_threshold(s1, s2, v1_ref, n1_ref, v2_ref, n2_ref):
    n = s1.shape[1]
    _top_values(s1, v1_ref, n1_ref)
    _top_values(s2, v2_ref, n2_ref)
    cand = _candidate_blocks(v1_ref, v2_ref, lambda a, b: a + b)
    wgt = _candidate_blocks(n1_ref, n2_ref, lambda a, b: a * b)
    work = cand
    cnt = jnp.zeros((1, n), F32)
    thr = jnp.zeros((1, n), F32)
    for _ in range(PEER_TOPK):
        m = jnp.max(work, axis=0, keepdims=True)
        eq = work == m
        thr = jnp.where(cnt < PEER_TOPK, m, thr)
        cnt = cnt + jnp.sum(jnp.where(eq, wgt, 0.0), axis=0, keepdims=True)
        work = jnp.where(eq, -jnp.inf, work)
    m1 = v1_ref[0:1, :]
    m2 = v2_ref[0:1, :]
    z = jnp.sum(jnp.where(cand >= thr, wgt * jnp.exp(cand - (m1 + m2)), 0.0), axis=0, keepdims=True)
    return thr, m1, m2, z


def _router_kernel(x_ref, g_ref, sh_ref, sc_ref, wq_ref, k1_ref, k2_ref,
                   h_ref, s1_ref, s2_ref, aux_ref,
                   v1_ref, n1_ref, v2_ref, n2_ref):
    tt, d = x_ref.shape

    @pl.when(pl.program_id(1) == 0)
    def _():
        _norm_mod_into(x_ref, g_ref, sh_ref, sc_ref, h_ref)

    q = jnp.dot(h_ref[...], wq_ref[...], preferred_element_type=F32).astype(BF16)
    nt = (((1,), (1,)), ((), ()))
    s1 = lax.dot_general(k1_ref[...], q[:, :PEER_HALF], nt, preferred_element_type=F32)
    s2 = lax.dot_general(k2_ref[...], q[:, PEER_HALF:], nt, preferred_element_type=F32)
    s1_ref[0] = s1
    s2_ref[0] = s2

    thr, m1, m2, z = _pair_threshold(s1, s2, v1_ref, n1_ref, v2_ref, n2_ref)
    aux_ref[0, 0:1, :] = thr
    aux_ref[0, 1:2, :] = m2
    aux_ref[0, 2:3, :] = m1 + jnp.log(z)
    aux_ref[0, 3:V7X_SUBLANES, :] = jnp.zeros((V7X_SUBLANES - 3, tt), F32)


def _router(x, mod, rows, norm_g, wq_bf, k1_bf, k2_bf):
    t, d = x.shape
    tt = _tile(rows.span, 512)
    row_of = rows.row_of
    nh = PEER_HEADS
    nk = PEER_NKEYS
    return pl.pallas_call(
        _router_kernel,
        grid=(t // tt, nh),
        in_specs=[pl.BlockSpec((tt, d), lambda i, j: (i, 0)),
                  pl.BlockSpec((1, d), lambda i, j: (0, 0)),
                  _mod_spec(d, 3, lambda i: row_of(i * tt)), _mod_spec(d, 4, lambda i: row_of(i * tt)),
                  pl.BlockSpec((d, PEER_QDIM), lambda i, j: (0, j)),
                  pl.BlockSpec((nk, PEER_HALF), lambda i, j: (0, 0)),
                  pl.BlockSpec((nk, PEER_HALF), lambda i, j: (0, 0))],
        out_specs=[pl.BlockSpec((tt, d), lambda i, j: (i, 0)),
                   pl.BlockSpec((1, nk, tt), lambda i, j: (j, 0, i)),
                   pl.BlockSpec((1, nk, tt), lambda i, j: (j, 0, i)),
                   pl.BlockSpec((1, V7X_SUBLANES, tt), lambda i, j: (j, 0, i))],
        out_shape=[jax.ShapeDtypeStruct((t, d), BF16),
                   jax.ShapeDtypeStruct((nh, nk, t), F32),
                   jax.ShapeDtypeStruct((nh, nk, t), F32),
                   jax.ShapeDtypeStruct((nh, V7X_SUBLANES, t), F32)],
        scratch_shapes=[pltpu.VMEM((PEER_TOPK, tt), F32)] * 4,
        compiler_params=_params(("parallel", "arbitrary")),
        name="peer_router",
    )(x, norm_g.reshape(1, d), mod, mod, wq_bf, k1_bf, k2_bf)


def _half_gate_rows(n1, s1_ref, s2_ref, e2_ref, aux_ref):
    w = None
    for h in range(s1_ref.shape[0]):
        s1row = s1_ref[h, pl.ds(n1, 1), :]
        g1row = 0.5 * jnp.exp(s1row - aux_ref[h, 2:3, :])
        sel = (s1row + s2_ref[h]) >= aux_ref[h, 0:1, :]
        term = jnp.where(sel, g1row * e2_ref[h], 0.0)
        w = term if w is None else w + term
    return w


def _expert_kernel(h_ref, u_ref, v_ref, s1_ref, s2_ref, aux_ref, o_ref, e2_ref):
    te = u_ref.shape[0]
    nh = s1_ref.shape[0]
    e = pl.program_id(1)
    blocks = te // PEER_NKEYS

    @pl.when(e == 0)
    def _():
        o_ref[...] = jnp.zeros(o_ref.shape, o_ref.dtype)
        for h in range(nh):
            e2_ref[h] = jnp.exp(s2_ref[h] - aux_ref[h, 1:2, :])

    nt = (((1,), (1,)), ((), ()))
    x = lax.dot_general(h_ref[...], u_ref[...], nt, preferred_element_type=F32)
    c = math.sqrt(2.0 / math.pi)
    t = jnp.tanh(x * (c + (c * 0.044715) * (x * x)))
    half_w = jnp.concatenate(
        [_half_gate_rows(e * blocks + r, s1_ref, s2_ref, e2_ref, aux_ref).T for r in range(blocks)], axis=1)
    p = ((x + x * t) * half_w).astype(BF16)
    o_ref[...] += jnp.dot(p, v_ref[...], preferred_element_type=F32)


def _experts(h, s1, s2, aux, u_bf, v_bf):
    t, d = h.shape
    ne = u_bf.shape[0]
    nh, nk, _ = s1.shape
    tt = _tile(t, 512)
    te = _tile(ne, 512)
    return pl.pallas_call(
        _expert_kernel,
        grid=(t // tt, ne // te),
        in_specs=[pl.BlockSpec((tt, d), lambda i, e: (i, 0)),
                  pl.BlockSpec((te, d), lambda i, e: (e, 0)),
                  pl.BlockSpec((te, d), lambda i, e: (e, 0)),
                  pl.BlockSpec((nh, nk, tt), lambda i, e: (0, 0, i)),
                  pl.BlockSpec((nh, nk, tt), lambda i, e: (0, 0, i)),
                  pl.BlockSpec((nh, V7X_SUBLANES, tt), lambda i, e: (0, 0, i))],
        out_specs=pl.BlockSpec((tt, d), lambda i, e: (i, 0)),
        out_shape=jax.ShapeDtypeStruct((t, d), F32),
        scratch_shapes=[pltpu.VMEM((nh, nk, tt), F32)],
        compiler_params=_params(("parallel", "arbitrary")),
        name="peer_experts",
    )(h, u_bf, v_bf, s1, s2, aux)


def _residual_kernel(x_ref, f_ref, gate_ref, o_ref):
    o_ref[...] = x_ref[...] + gate_ref[0] * f_ref[...]


def _residual(x, f, mod, rows, chunk):
    t, d = x.shape
    tm = _tile(rows.span, 256)
    row_of = rows.row_of
    return pl.pallas_call(
        _residual_kernel,
        grid=(t // tm,),
        in_specs=[pl.BlockSpec((tm, d), lambda i: (i, 0)),
                  pl.BlockSpec((tm, d), lambda i: (i, 0)),
                  pl.BlockSpec((1, 1, d), lambda i: (row_of(i * tm), 0, chunk))],
        out_specs=pl.BlockSpec((tm, d), lambda i: (i, 0)),
        out_shape=jax.ShapeDtypeStruct((t, d), F32),
        compiler_params=_params(("parallel",)),
        name="gated_residual",
    )(x, f, mod)


def _peer_layer(x, mod, rows, norm_g, wq_bf, k1_bf, k2_bf, u_bf, v_bf):
    h, s1, s2, aux = _router(x, mod, rows, norm_g, wq_bf, k1_bf, k2_bf)
    f = _experts(h, s1, s2, aux, u_bf, v_bf)
    return _residual(x, f, mod, rows, 5)


def _norm_mod_kernel(x_ref, g_ref, sh_ref, sc_ref, h_ref):
    _norm_mod_into(x_ref, g_ref, sh_ref, sc_ref, h_ref)


def _norm_mod(x, mod, rows, norm_g):
    t, d = x.shape
    tm = _tile(rows.span, 256)
    row_of = rows.row_of
    spec = lambda chunk: pl.BlockSpec((1, 1, d), lambda i: (row_of(i * tm), 0, chunk))
    return pl.pallas_call(
        _norm_mod_kernel,
        grid=(t // tm,),
        in_specs=[pl.BlockSpec((tm, d), lambda i: (i, 0)),
                  pl.BlockSpec((1, d), lambda i: (0, 0)),
                  spec(0), spec(1)],
        out_specs=pl.BlockSpec((tm, d), lambda i: (i, 0)),
        out_shape=jax.ShapeDtypeStruct((t, d), BF16),
        compiler_params=_params(("parallel",)),
        name="norm_modulate",
    )(x, norm_g.reshape(1, d), mod, mod)


def _qkv_kernel(rotary, n_qk, h_ref, w_ref, qg_ref, kg_ref, cos_ref, sa_ref, sb_ref, o_ref):
    j = pl.program_id(1)
    acc = jnp.dot(h_ref[...], w_ref[...], preferred_element_type=F32)
    tn = acc.shape[1]

    def qk_norm(gain_ref, scale):
        cols = []
        for c in range(tn // DA_HEAD_DIM):
            t = acc[:, c * DA_HEAD_DIM:(c + 1) * DA_HEAD_DIM]
            t = t * lax.rsqrt(jnp.mean(t * t, axis=-1, keepdims=True) + EPS) * gain_ref[...]
            if rotary:
                t = (t * cos_ref[...]
                     + pltpu.roll(t, DA_HEAD_DIM - ROPE_FREQS, axis=1) * sa_ref[...]
                     + pltpu.roll(t, ROPE_FREQS, axis=1) * sb_ref[...])
            cols.append(t * scale if scale != 1.0 else t)
        o_ref[...] = jnp.concatenate(cols, axis=1).astype(o_ref.dtype)

    @pl.when(j < n_qk)
    def _():
        qk_norm(qg_ref, DA_SCALE * LOG2E)

    @pl.when(jnp.logical_and(j >= n_qk, j < 2 * n_qk))
    def _():
        qk_norm(kg_ref, 1.0)

    @pl.when(j >= 2 * n_qk)
    def _():
        o_ref[...] = acc.astype(o_ref.dtype)


def _rotary_tables(seq_len):
    t = jnp.arange(seq_len, dtype=jnp.int32)
    row = (t // GRID_W).astype(F32)
    col = (t % GRID_W).astype(F32)
    inv = ROPE_BASE ** (-jnp.arange(ROPE_FREQS, dtype=F32) / ROPE_FREQS)
    ar = row[:, None] * inv
    ac = col[:, None] * inv
    ang = jnp.concatenate([ar, ar, ac, ac], axis=-1)
    cos, sin = jnp.cos(ang), jnp.sin(ang)
    first_half = (jnp.arange(DA_HEAD_DIM) % (2 * ROPE_FREQS)) < ROPE_FREQS
    sin_a = jnp.where(first_half, -sin, 0.0)
    sin_b = jnp.where(first_half, 0.0, sin)
    return cos, sin_a, sin_b


def _qkv(x, mod, rows, seq_len, norm_g, w_bf, q_gain, k_gain, rotary):
    t, d = x.shape
    h = _norm_mod(x, mod, rows, norm_g)
    tm = _tile(seq_len, 1024)
    tn = _tile(d, 512)
    n_qk = d // tn
    tiles_per_seq = seq_len // tm
    cos, sin_a, sin_b = _rotary_tables(seq_len)
    tab_spec = pl.BlockSpec((tm, DA_HEAD_DIM), lambda i, j: (i % tiles_per_seq, 0))
    return pl.pallas_call(
        functools.partial(_qkv_kernel, rotary, n_qk),
        grid=(t // tm, 3 * n_qk),
        in_specs=[pl.BlockSpec((tm, d), lambda i, j: (i, 0)),
                  pl.BlockSpec((d, tn), lambda i, j: (0, j)),
                  pl.BlockSpec((1, DA_HEAD_DIM), lambda i, j: (0, 0)),
                  pl.BlockSpec((1, DA_HEAD_DIM), lambda i, j: (0, 0)),
                  tab_spec, tab_spec, tab_spec],
        out_specs=pl.BlockSpec((tm, tn), lambda i, j: (i, j)),
        out_shape=jax.ShapeDtypeStruct((t, 3 * d), BF16),
        compiler_params=_params(("parallel", "arbitrary")),
        name="qkv_proj",
    )(h, w_bf, q_gain.reshape(1, -1), k_gain.reshape(1, -1), cos, sin_a, sin_b)


def _attn_kernel(lam_init, q_ref, kl_ref, vl_ref, kc_ref, vc_ref, lam_ref, sg_ref, o_ref):
    hd = DA_HEAD_DIM
    lq1, lk1, lq2, lk2 = (lam_ref[r:r + 1, :] for r in range(4))
    lam = (jnp.exp(jnp.sum(lq1 * lk1, axis=-1, keepdims=True))
           - jnp.exp(jnp.sum(lq2 * lk2, axis=-1, keepdims=True)) + lam_init)
    nt = (((1,), (1,)), ((), ()))
    v_lat = vl_ref[...]
    v_ctx = vc_ref[...]

    def attend(sub):
        q = q_ref[:, sub * hd:(sub + 1) * hd]
        s_lat = lax.dot_general(q, kl_ref[:, sub * hd:(sub + 1) * hd], nt, preferred_element_type=F32)
        s_ctx = lax.dot_general(q, kc_ref[:, sub * hd:(sub + 1) * hd], nt, preferred_element_type=F32)
        m = jnp.maximum(jnp.max(s_lat, axis=-1, keepdims=True), jnp.max(s_ctx, axis=-1, keepdims=True))
        p_lat = jnp.exp2(s_lat - m)
        p_ctx = jnp.exp2(s_ctx - m)
        z = jnp.sum(p_lat, axis=-1, keepdims=True) + jnp.sum(p_ctx, axis=-1, keepdims=True)
        o = (jnp.dot(p_lat.astype(BF16), v_lat, preferred_element_type=F32)
             + jnp.dot(p_ctx.astype(BF16), v_ctx, preferred_element_type=F32))
        return o / z

    o = attend(0) - lam * attend(1)
    o = o * lax.rsqrt(jnp.mean(o * o, axis=-1, keepdims=True) + EPS) * sg_ref[...]
    o_ref[...] = (o * (1.0 - lam_init)).astype(o_ref.dtype)


def _attention(qkv_lat, qkv_ctx, batch, lam_rows, subln_g, lam_init):
    t, d3 = qkv_lat.shape
    d = d3 // 3
    nh = d // DA_VDIM
    seq = t // batch
    ctx_len = qkv_ctx.shape[0] // batch
    tq = _tile(seq, 512)
    nq = seq // tq
    return pl.pallas_call(
        functools.partial(_attn_kernel, lam_init),
        grid=(batch, nh, nq),
        in_specs=[pl.BlockSpec((tq, DA_VDIM), lambda b, h, i: (b * nq + i, h)),
                  pl.BlockSpec((seq, DA_VDIM), lambda b, h, i: (b, nh + h)),
                  pl.BlockSpec((seq, DA_VDIM), lambda b, h, i: (b, 2 * nh + h)),
                  pl.BlockSpec((ctx_len, DA_VDIM), lambda b, h, i: (b, nh + h)),
                  pl.BlockSpec((ctx_len, DA_VDIM), lambda b, h, i: (b, 2 * nh + h)),
                  pl.BlockSpec((4, DA_HEAD_DIM), lambda b, h, i: (0, 0)),
                  pl.BlockSpec((1, DA_VDIM), lambda b, h, i: (0, 0))],
        out_specs=pl.BlockSpec((tq, DA_VDIM), lambda b, h, i: (b * nq + i, h)),
        out_shape=jax.ShapeDtypeStruct((t, d), BF16),
        compiler_params=_params(("parallel", "parallel", "arbitrary")),
        name="diff_attention",
    )(qkv_lat, qkv_lat, qkv_lat, qkv_ctx, qkv_ctx, lam_rows, subln_g.reshape(1, DA_VDIM))


def _oproj_kernel(a_ref, w_ref, x_ref, gate_ref, o_ref):
    y = jnp.dot(a_ref[...], w_ref[...], preferred_element_type=F32)
    o_ref[...] = x_ref[...] + gate_ref[0] * y


def _oproj(a, w_bf, x, mod, rows):
    t, d = x.shape
    tm = _tile(rows.span, 1024)
    row_of = rows.row_of
    tn = _tile(d, 512)
    nj = d // tn
    return pl.pallas_call(
        _oproj_kernel,
        grid=(t // tm, nj),
        in_specs=[pl.BlockSpec((tm, d), lambda i, j: (i, 0)),
                  pl.BlockSpec((d, tn), lambda i, j: (0, j)),
                  pl.BlockSpec((tm, tn), lambda i, j: (i, j)),
                  pl.BlockSpec((1, 1, tn), lambda i, j: (row_of(i * tm), 0, 2 * nj + j))],
        out_specs=pl.BlockSpec((tm, tn), lambda i, j: (i, j)),
        out_shape=jax.ShapeDtypeStruct((t, d), F32),
        compiler_params=_params(("parallel", "arbitrary")),
        name="attn_out_proj",
    )(a, w_bf, x, mod)


def kernel(x, c, ctx, c_ctx, l0_ada_w, l0_ada_b, l0_norm1_g, l0_glu_w, l0_glu_b, l0_dw_w, l0_dw_b, l0_ln_g, l0_ln_b, l0_pw_w, l0_pw_b, l0_norm2_g, l0_peer_wq, l0_peer_k1, l0_peer_k2, l0_peer_u, l0_peer_v, l1_ada_w, l1_ada_b, l1_norm1_g, l1_qkv_w, l1_q_norm_g, l1_k_norm_g, l1_lam_q1, l1_lam_k1, l1_lam_q2, l1_lam_k2, l1_subln_g, l1_o_w, l1_norm2_g, l1_peer_wq, l1_peer_k1, l1_peer_k2, l1_peer_u, l1_peer_v):
    batch, seq, d = x.shape
    ctx_len = ctx.shape[1]
    assert seq % GRID_W == 0 and d % DA_VDIM == 0 and batch < 8
    xl = x.reshape(batch * seq, d)
    xc = ctx.reshape(batch * ctx_len, d)
    ctx_row = batch
    cvec = jnp.concatenate([c, c_ctx[None, :], jnp.zeros((8 - batch - 1, d), F32)], axis=0)

    lat_row = ModRows(lambda tok: tok // seq, seq)
    ctx_row_of = ModRows(lambda tok: ctx_row, batch * ctx_len)

    bf = lambda w: w.astype(BF16)

    mod0 = _adaln(cvec, l0_ada_w, l0_ada_b)
    glu_bf, pw_bf = bf(l0_glu_w), bf(l0_pw_w)
    peer0 = (bf(l0_peer_wq), bf(l0_peer_k1), bf(l0_peer_k2), bf(l0_peer_u), bf(l0_peer_v))
    conv_args = (l0_dw_w, l0_dw_b, l0_ln_g, l0_ln_b, pw_bf, l0_pw_b)

    def layer0(xs, row_of, seq_len):
        a = _glu(xs, mod0, row_of, l0_norm1_g, glu_bf, l0_glu_b)
        x1 = _conv_pw(a, xs, mod0, row_of, seq_len, *conv_args)
        return _peer_layer(x1, mod0, row_of, l0_norm2_g, *peer0)

    xl = layer0(xl, lat_row, seq)
    xc = layer0(xc, ctx_row_of, ctx_len)

    mod1 = _adaln(cvec, l1_ada_w, l1_ada_b)
    lam_init = 0.8 - 0.6 * math.exp(-0.3 * 1)
    qkv_bf = bf(l1_qkv_w)
    qkv_lat = _qkv(xl, mod1, lat_row, seq, l1_norm1_g, qkv_bf, l1_q_norm_g, l1_k_norm_g, True)
    qkv_ctx = _qkv(xc, mod1, ctx_row_of, ctx_len, l1_norm1_g, qkv_bf, l1_q_norm_g, l1_k_norm_g, False)
    lam_rows = jnp.stack([l1_lam_q1, l1_lam_k1, l1_lam_q2, l1_lam_k2], axis=0)
    o = _attention(qkv_lat, qkv_ctx, batch, lam_rows, l1_subln_g, lam_init)
    xl = _oproj(o, bf(l1_o_w), xl, mod1, lat_row)
    peer1 = (bf(l1_peer_wq), bf(l1_peer_k1), bf(l1_peer_k2), bf(l1_peer_u), bf(l1_peer_v))
    xl = _peer_layer(xl, mod1, lat_row, l1_norm2_g, *peer1)
    return xl.reshape(batch, seq, d)
```

```python
import functools
import math
from typing import Callable, NamedTuple

import jax
import jax.numpy as jnp
from jax import lax
from jax.experimental import pallas as pl
from jax.experimental.pallas import tpu as pltpu

F32 = jnp.float32
BF16 = jnp.bfloat16

GRID_W = 64
CONV_WIDTH = 31
CONV_PAD = (CONV_WIDTH - 1) // 2
DA_HEAD_DIM = 128
DA_VDIM = 2 * DA_HEAD_DIM
DA_SCALE = DA_HEAD_DIM ** -0.5
ROPE_BASE = 10000.0
ROPE_FREQS = DA_HEAD_DIM // 4
PEER_HEADS = 8
PEER_NKEYS = 128
PEER_QDIM = 256
PEER_HALF = PEER_QDIM // 2
PEER_TOPK = 16
EPS = 1e-6
LOG2E = math.log2(math.e)

V7X_LANES = 128
V7X_SUBLANES = 8
V7X_BF16_ROWS = 16
V7X_VMEM_BYTES = 64 * 1024 * 1024
VMEM_LIMIT = V7X_VMEM_BYTES - 8 * 1024 * 1024

ROW_CHUNK = 64
COL_CHUNK = 512
ROUTER_HEADS = 2
HALO = 16


def _tile(n, pref):
    t = min(n, pref)
    while n % t:
        t //= 2
    return t


def _params(sem):
    return pltpu.CompilerParams(dimension_semantics=sem, vmem_limit_bytes=VMEM_LIMIT)


def _ada_kernel(c_ref, w_ref, b_ref, o_ref):
    c = c_ref[...]
    s = (c * jax.nn.sigmoid(c)).astype(BF16)
    o_ref[...] = jnp.dot(s, w_ref[...].astype(BF16), preferred_element_type=F32) + b_ref[...]


def _adaln(cvec, w, b):
    d, n = w.shape
    tn = _tile(n, 512)
    out = pl.pallas_call(
        _ada_kernel,
        grid=(n // tn,),
        in_specs=[pl.BlockSpec((8, d), lambda j: (0, 0)),
                  pl.BlockSpec((d, tn), lambda j: (0, j)),
                  pl.BlockSpec((1, tn), lambda j: (0, j))],
        out_specs=pl.BlockSpec((8, tn), lambda j: (0, j)),
        out_shape=jax.ShapeDtypeStruct((8, n), F32),
        compiler_params=_params(("arbitrary",)),
        name="adaln",
    )(cvec, w, b.reshape(1, n))
    return out.reshape(8, 1, n)


def _norm_mod_into(x_ref, g_ref, sh_ref, sc_ref, h_ref):
    rows, d = x_ref.shape
    sub_rows = V7X_BF16_ROWS
    cc = min(d, COL_CHUNK)

    def body(r, carry):
        for s in range(ROW_CHUNK // sub_rows):
            sl = pl.ds(pl.multiple_of(r * ROW_CHUNK + s * sub_rows, sub_rows), sub_rows)
            part = None
            for c in range(d // cc):
                xs = x_ref[sl, c * cc:(c + 1) * cc]
                sq = xs * xs
                for lb in range(cc // V7X_LANES):
                    piece = sq[:, lb * V7X_LANES:(lb + 1) * V7X_LANES]
                    part = piece if part is None else part + piece
            inv = lax.rsqrt(jnp.sum(part, axis=-1, keepdims=True) * (1.0 / d) + EPS)
            for c in range(d // cc):
                cols = slice(c * cc, (c + 1) * cc)
                y = x_ref[sl, cols] * inv
                h_ref[sl, cols] = ((y * g_ref[:, cols]) * (1.0 + sc_ref[0, :, cols]) + sh_ref[0, :, cols]
                                   ).astype(h_ref.dtype)
        return carry

    lax.fori_loop(0, rows // ROW_CHUNK, body, 0)


class ModRows(NamedTuple):
    row_of: Callable
    span: int


def _mod_spec(d, chunk, row_of):
    return pl.BlockSpec((1, 1, d), lambda i, j: (row_of(i), 0, chunk))


def _glu_kernel(x_ref, g_ref, sh_ref, sc_ref, wl_ref, wr_ref, bl_ref, br_ref, o_ref, h_ref):
    @pl.when(pl.program_id(1) == 0)
    def _():
        _norm_mod_into(x_ref, g_ref, sh_ref, sc_ref, h_ref)

    h = h_ref[...]
    left = jnp.dot(h, wl_ref[...], preferred_element_type=F32) + bl_ref[...]
    right = jnp.dot(h, wr_ref[...], preferred_element_type=F32) + br_ref[...]
    o_ref[...] = (left * jax.nn.sigmoid(right)).astype(o_ref.dtype)


def _glu(x, mod, rows, norm_g, w_bf, b):
    t, d = x.shape
    tm = _tile(rows.span, 512)
    row_of = rows.row_of
    tn = _tile(d, 512)
    nj = d // tn
    return pl.pallas_call(
        _glu_kernel,
        grid=(t // tm, nj),
        in_specs=[pl.BlockSpec((tm, d), lambda i, j: (i, 0)),
                  pl.BlockSpec((1, d), lambda i, j: (0, 0)),
                  _mod_spec(d, 0, lambda i: row_of(i * tm)), _mod_spec(d, 1, lambda i: row_of(i * tm)),
                  pl.BlockSpec((d, tn), lambda i, j: (0, j)),
                  pl.BlockSpec((d, tn), lambda i, j: (0, j + nj)),
                  pl.BlockSpec((1, tn), lambda i, j: (0, j)),
                  pl.BlockSpec((1, tn), lambda i, j: (0, j + nj))],
        out_specs=pl.BlockSpec((tm, tn), lambda i, j: (i, j)),
        out_shape=jax.ShapeDtypeStruct((t, d), BF16),
        scratch_shapes=[pltpu.VMEM((tm, d), BF16)],
        compiler_params=_params(("parallel", "arbitrary")),
        name="glu_proj",
    )(x, norm_g.reshape(1, d), mod, mod, w_bf, w_bf, b.reshape(1, 2 * d), b.reshape(1, 2 * d))


def _conv_pw_kernel(tiles_per_seq, ap_ref, am_ref, an_ref, dw_ref, dwb_ref, lng_ref, lnb_ref,
                    pw_ref, pwb_ref, x_ref, gate_ref, o_ref, abuf_ref, y_ref, h_ref, sh_ref):
    tm, d = am_ref.shape

    @pl.when(pl.program_id(1) == 0)
    def _():
        pos = pl.program_id(0) % tiles_per_seq
        prev = ap_ref[...].astype(F32)
        nxt = an_ref[...].astype(F32)
        abuf_ref[0:HALO, :] = jnp.where(pos == 0, 0.0, prev)
        abuf_ref[HALO:HALO + tm, :] = am_ref[...].astype(F32)
        abuf_ref[HALO + tm:HALO + tm + HALO, :] = jnp.where(pos == tiles_per_seq - 1, 0.0, nxt)

        first = HALO - CONV_PAD
        sub = V7X_SUBLANES

        def conv_body(c, carry):
            lanes = pl.ds(pl.multiple_of(c * V7X_LANES, V7X_LANES), V7X_LANES)
            col = abuf_ref[:, lanes]
            n_keep = tm + 2 * HALO - sub
            for r in range(sub):
                sh_ref[r, 0:n_keep, :] = col[r:r + n_keep, :]
            taps = [dw_ref[k * sub:(k + 1) * sub, lanes] for k in range(CONV_WIDTH)]
            bias = jnp.broadcast_to(dwb_ref[:, lanes], (sub, V7X_LANES))
            for g in range(tm // sub):
                acc = bias
                for k in range(CONV_WIDTH):
                    q, r = divmod(first + k, sub)
                    acc = acc + taps[k] * sh_ref[r, (g + q) * sub:(g + q + 1) * sub, :]
                y_ref[g * sub:(g + 1) * sub, lanes] = acc
            return carry

        lax.fori_loop(0, d // V7X_LANES, conv_body, 0)

        sub_rows = V7X_BF16_ROWS
        cc = min(d, COL_CHUNK)

        def row_sum(sl, fn):
            part = None
            for c in range(d // cc):
                blk = fn(y_ref[sl, c * cc:(c + 1) * cc])
                for lb in range(cc // V7X_LANES):
                    piece = blk[:, lb * V7X_LANES:(lb + 1) * V7X_LANES]
                    part = piece if part is None else part + piece
            return jnp.sum(part, axis=-1, keepdims=True)

        def ln_body(r, carry):
            for s in range(ROW_CHUNK // sub_rows):
                sl = pl.ds(pl.multiple_of(r * ROW_CHUNK + s * sub_rows, sub_rows), sub_rows)
                mu = row_sum(sl, lambda b: b) * (1.0 / d)
                inv = lax.rsqrt(row_sum(sl, lambda b: (b - mu) * (b - mu)) * (1.0 / d) + EPS)
                for c in range(d // cc):
                    cols = slice(c * cc, (c + 1) * cc)
                    z = ((y_ref[sl, cols] - mu) * inv) * lng_ref[:, cols] + lnb_ref[:, cols]
                    h_ref[sl, cols] = (z * jax.nn.sigmoid(z)).astype(h_ref.dtype)
            return carry

        lax.fori_loop(0, tm // ROW_CHUNK, ln_body, 0)

    y = jnp.dot(h_ref[...], pw_ref[...], preferred_element_type=F32) + pwb_ref[...]
    o_ref[...] = x_ref[...] + gate_ref[0] * y


def _conv_pw(a, x, mod, rows, seq_len, dw_w, dw_b, ln_g, ln_b, pw_bf, pw_b):
    t, d = a.shape
    row_of = rows.row_of
    tm = _tile(seq_len, 512)
    tn = _tile(d, 512)
    nj = d // tn
    tiles_per_seq = seq_len // tm
    hb = tm // HALO
    n_hb = t // HALO
    return pl.pallas_call(
        functools.partial(_conv_pw_kernel, tiles_per_seq),
        grid=(t // tm, nj),
        in_specs=[pl.BlockSpec((HALO, d), lambda i, j: (jnp.maximum(i * hb - 1, 0), 0)),
                  pl.BlockSpec((tm, d), lambda i, j: (i, 0)),
                  pl.BlockSpec((HALO, d), lambda i, j: (jnp.minimum((i + 1) * hb, n_hb - 1), 0)),
                  pl.BlockSpec((CONV_WIDTH * V7X_SUBLANES, d), lambda i, j: (0, 0)),
                  pl.BlockSpec((1, d), lambda i, j: (0, 0)),
                  pl.BlockSpec((1, d), lambda i, j: (0, 0)),
                  pl.BlockSpec((1, d), lambda i, j: (0, 0)),
                  pl.BlockSpec((d, tn), lambda i, j: (0, j)),
                  pl.BlockSpec((1, tn), lambda i, j: (0, j)),
                  pl.BlockSpec((tm, tn), lambda i, j: (i, j)),
                  pl.BlockSpec((1, 1, tn), lambda i, j: (row_of(i * tm), 0, 2 * nj + j))],
        out_specs=pl.BlockSpec((tm, tn), lambda i, j: (i, j)),
        out_shape=jax.ShapeDtypeStruct((t, d), F32),
        scratch_shapes=[pltpu.VMEM((tm + 2 * HALO, d), F32),
                        pltpu.VMEM((tm, d), F32),
                        pltpu.VMEM((tm, d), BF16),
                        pltpu.VMEM((V7X_SUBLANES, tm + 2 * HALO, V7X_LANES), F32)],
        compiler_params=_params(("parallel", "arbitrary")),
        name="conv_ln_pw",
    )(a, a, a, jnp.repeat(dw_w, V7X_SUBLANES, axis=0), dw_b.reshape(1, d), ln_g.reshape(1, d), ln_b.reshape(1, d),
      pw_bf, pw_b.reshape(1, d), x, mod)


def _merge_exchange_pairs(n):
    pairs = []
    p = 1
    while p < n:
        k = p
        while k >= 1:
            for j in range(k % p, n - k, 2 * k):
                for i in range(min(k, n - j - k)):
                    if (i + j) // (2 * p) == (i + j + k) // (2 * p):
                        pairs.append((i + j, i + j + k))
            k //= 2
        p *= 2
    return pairs


def _top_values(s, val_ref, cnt_ref):
    groups = s.shape[0] // V7X_SUBLANES
    assert groups == PEER_TOPK
    lists = [s[g * V7X_SUBLANES:(g + 1) * V7X_SUBLANES, :] for g in range(groups)]
    for i, j in _merge_exchange_pairs(groups):
        hi, lo = jnp.maximum(lists[i], lists[j]), jnp.minimum(lists[i], lists[j])
        lists[i], lists[j] = hi, lo
    for it in range(PEER_TOPK):
        m = jnp.max(lists[0], axis=0, keepdims=True)
        eq = lists[0] == m
        val_ref[it:it + 1, :] = m
        cnt_ref[it:it + 1, :] = jnp.sum(jnp.where(eq, 1.0, 0.0), axis=0, keepdims=True)
        for g in range(PEER_TOPK - it - 1):
            lists[g] = jnp.where(eq, lists[g + 1], lists[g])


def _candidate_blocks(a_ref, b_ref, combine):
    half = PEER_TOPK // 2
    blocks = [combine(a_ref[0:1, :], b_ref[...]),
              combine(a_ref[half:PEER_TOPK, :], b_ref[0:1, :])]
    for a in range(1, half):
        blocks.append(combine(a_ref[a:a + 1, :], b_ref[0:half, :]))
    return jnp.concatenate(blocks, axis=0)


def _pair_threshold(s1, s2, v1_ref, n1_ref, v2_ref, n2_ref):
    n = s1.shape[1]
    _top_values(s1, v1_ref, n1_ref)
    _top_values(s2, v2_ref, n2_ref)
    cand = _candidate_blocks(v1_ref, v2_ref, lambda a, b: a + b)
    wgt = _candidate_blocks(n1_ref, n2_ref, lambda a, b: a * b)
    work = cand
    cnt = jnp.zeros((1, n), F32)
    thr = jnp.zeros((1, n), F32)
    for _ in range(PEER_TOPK):
        m = jnp.max(work, axis=0, keepdims=True)
        eq = work == m
        thr = jnp.where(cnt < PEER_TOPK, m, thr)
        cnt = cnt + jnp.sum(jnp.where(eq, wgt, 0.0), axis=0, keepdims=True)
        work = jnp.where(eq, -jnp.inf, work)
    m1 = v1_ref[0:1, :]
    m2 = v2_ref[0:1, :]
    z = jnp.sum(jnp.where(cand >= thr, wgt * jnp.exp(cand - (m1 + m2)), 0.0), axis=0, keepdims=True)
    return thr, m1, m2, z


def _router_kernel(x_ref, g_ref, sh_ref, sc_ref, wq_ref, k1_ref, k2_ref,
                   h_ref, s1_ref, s2_ref, aux_ref,
                   v1_ref, n1_ref, v2_ref, n2_ref):
    tt, d = x_ref.shape

    @pl.when(pl.program_id(1) == 0)
    def _():
        _norm_mod_into(x_ref, g_ref, sh_ref, sc_ref, h_ref)

    q = jnp.dot(h_ref[...], wq_ref[...], preferred_element_type=F32).astype(BF16)
    nt = (((1,), (1,)), ((), ()))
    for hh in range(ROUTER_HEADS):
        q1 = q[:, hh * PEER_QDIM:hh * PEER_QDIM + PEER_HALF]
        q2 = q[:, hh * PEER_QDIM + PEER_HALF:(hh + 1) * PEER_QDIM]
        s1 = lax.dot_general(k1_ref[...], q1, nt, preferred_element_type=F32)
        s2 = lax.dot_general(k2_ref[...], q2, nt, preferred_element_type=F32)
        s1_ref[hh] = s1
        s2_ref[hh] = s2
        thr, m1, m2, z = _pair_threshold(s1, s2, v1_ref, n1_ref, v2_ref, n2_ref)
        aux_ref[hh, 0:1, :] = thr
        aux_ref[hh, 1:2, :] = m2
        aux_ref[hh, 2:3, :] = m1 + jnp.log(z)
        aux_ref[hh, 3:V7X_SUBLANES, :] = jnp.zeros((V7X_SUBLANES - 3, tt), F32)


def _router(x, mod, rows, norm_g, wq_bf, k1_bf, k2_bf):
    t, d = x.shape
    tt = _tile(rows.span, 512)
    row_of = rows.row_of
    nh = PEER_HEADS
    nk = PEER_NKEYS
    return pl.pallas_call(
        _router_kernel,
        grid=(t // tt, nh // ROUTER_HEADS),
        in_specs=[pl.BlockSpec((tt, d), lambda i, j: (i, 0)),
                  pl.BlockSpec((1, d), lambda i, j: (0, 0)),
                  _mod_spec(d, 3, lambda i: row_of(i * tt)), _mod_spec(d, 4, lambda i: row_of(i * tt)),
                  pl.BlockSpec((d, ROUTER_HEADS * PEER_QDIM), lambda i, j: (0, j)),
                  pl.BlockSpec((nk, PEER_HALF), lambda i, j: (0, 0)),
                  pl.BlockSpec((nk, PEER_HALF), lambda i, j: (0, 0))],
        out_specs=[pl.BlockSpec((tt, d), lambda i, j: (i, 0)),
                   pl.BlockSpec((ROUTER_HEADS, nk, tt), lambda i, j: (j, 0, i)),
                   pl.BlockSpec((ROUTER_HEADS, nk, tt), lambda i, j: (j, 0, i)),
                   pl.BlockSpec((ROUTER_HEADS, V7X_SUBLANES, tt), lambda i, j: (j, 0, i))],
        out_shape=[jax.ShapeDtypeStruct((t, d), BF16),
                   jax.ShapeDtypeStruct((nh, nk, t), F32),
                   jax.ShapeDtypeStruct((nh, nk, t), F32),
                   jax.ShapeDtypeStruct((nh, V7X_SUBLANES, t), F32)],
        scratch_shapes=[pltpu.VMEM((PEER_TOPK, tt), F32)] * 4,
        compiler_params=_params(("parallel", "arbitrary")),
        name="peer_router",
    )(x, norm_g.reshape(1, d), mod, mod, wq_bf, k1_bf, k2_bf)


def _half_gate_rows(n1, s1_ref, s2_ref, e2_ref, aux_ref):
    w = None
    for h in range(s1_ref.shape[0]):
        s1row = s1_ref[h, pl.ds(n1, 1), :]
        g1row = 0.5 * jnp.exp(s1row - aux_ref[h, 2:3, :])
        sel = (s1row + s2_ref[h]) >= aux_ref[h, 0:1, :]
        term = jnp.where(sel, g1row * e2_ref[h], 0.0)
        w = term if w is None else w + term
    return w


def _expert_kernel(h_ref, u_ref, v_ref, s1_ref, s2_ref, aux_ref, o_ref, e2_ref):
    te = u_ref.shape[0]
    nh = s1_ref.shape[0]
    e = pl.program_id(1)
    blocks = te // PEER_NKEYS

    @pl.when(e == 0)
    def _():
        o_ref[...] = jnp.zeros(o_ref.shape, o_ref.dtype)
        for h in range(nh):
            e2_ref[h] = jnp.exp(s2_ref[h] - aux_ref[h, 1:2, :])

    nt = (((1,), (1,)), ((), ()))
    x = lax.dot_general(h_ref[...], u_ref[...], nt, preferred_element_type=F32)
    c = math.sqrt(2.0 / math.pi)
    t = jnp.tanh(x * (c + (c * 0.044715) * (x * x)))
    half_w = jnp.concatenate(
        [_half_gate_rows(e * blocks + r, s1_ref, s2_ref, e2_ref, aux_ref).T for r in range(blocks)], axis=1)
    p = ((x + x * t) * half_w).astype(BF16)
    o_ref[...] += jnp.dot(p, v_ref[...], preferred_element_type=F32)


def _experts(h, s1, s2, aux, u_bf, v_bf):
    t, d = h.shape
    ne = u_bf.shape[0]
    nh, nk, _ = s1.shape
    tt = _tile(t, 512)
    te = _tile(ne, 512)
    return pl.pallas_call(
        _expert_kernel,
        grid=(t // tt, ne // te),
        in_specs=[pl.BlockSpec((tt, d), lambda i, e: (i, 0)),
                  pl.BlockSpec((te, d), lambda i, e: (e, 0)),
                  pl.BlockSpec((te, d), lambda i, e: (e, 0)),
                  pl.BlockSpec((nh, nk, tt), lambda i, e: (0, 0, i)),
                  pl.BlockSpec((nh, nk, tt), lambda i, e: (0, 0, i)),
                  pl.BlockSpec((nh, V7X_SUBLANES, tt), lambda i, e: (0, 0, i))],
        out_specs=pl.BlockSpec((tt, d), lambda i, e: (i, 0)),
        out_shape=jax.ShapeDtypeStruct((t, d), F32),
        scratch_shapes=[pltpu.VMEM((nh, nk, tt), F32)],
        compiler_params=_params(("parallel", "arbitrary")),
        name="peer_experts",
    )(h, u_bf, v_bf, s1, s2, aux)


def _residual_kernel(x_ref, f_ref, gate_ref, o_ref):
    o_ref[...] = x_ref[...] + gate_ref[0] * f_ref[...]


def _residual(x, f, mod, rows, chunk):
    t, d = x.shape
    tm = _tile(rows.span, 256)
    row_of = rows.row_of
    return pl.pallas_call(
        _residual_kernel,
        grid=(t // tm,),
        in_specs=[pl.BlockSpec((tm, d), lambda i: (i, 0)),
                  pl.BlockSpec((tm, d), lambda i: (i, 0)),
                  pl.BlockSpec((1, 1, d), lambda i: (row_of(i * tm), 0, chunk))],
        out_specs=pl.BlockSpec((tm, d), lambda i: (i, 0)),
        out_shape=jax.ShapeDtypeStruct((t, d), F32),
        compiler_params=_params(("parallel",)),
        name="gated_residual",
    )(x, f, mod)


def _peer_layer(x, mod, rows, norm_g, wq_bf, k1_bf, k2_bf, u_bf, v_bf):
    h, s1, s2, aux = _router(x, mod, rows, norm_g, wq_bf, k1_bf, k2_bf)
    f = _experts(h, s1, s2, aux, u_bf, v_bf)
    return _residual(x, f, mod, rows, 5)


def _norm_mod_kernel(x_ref, g_ref, sh_ref, sc_ref, h_ref):
    _norm_mod_into(x_ref, g_ref, sh_ref, sc_ref, h_ref)


def _norm_mod(x, mod, rows, norm_g):
    t, d = x.shape
    tm = _tile(rows.span, 256)
    row_of = rows.row_of
    spec = lambda chunk: pl.BlockSpec((1, 1, d), lambda i: (row_of(i * tm), 0, chunk))
    return pl.pallas_call(
        _norm_mod_kernel,
        grid=(t // tm,),
        in_specs=[pl.BlockSpec((tm, d), lambda i: (i, 0)),
                  pl.BlockSpec((1, d), lambda i: (0, 0)),
                  spec(0), spec(1)],
        out_specs=pl.BlockSpec((tm, d), lambda i: (i, 0)),
        out_shape=jax.ShapeDtypeStruct((t, d), BF16),
        compiler_params=_params(("parallel",)),
        name="norm_modulate",
    )(x, norm_g.reshape(1, d), mod, mod)


def _qkv_kernel(rotary, n_qk, h_ref, w_ref, qg_ref, kg_ref, cos_ref, sa_ref, sb_ref, o_ref):
    j = pl.program_id(1)
    acc = jnp.dot(h_ref[...], w_ref[...], preferred_element_type=F32)
    tn = acc.shape[1]

    def qk_norm(gain_ref, scale):
        cols = []
        for c in range(tn // DA_HEAD_DIM):
            t = acc[:, c * DA_HEAD_DIM:(c + 1) * DA_HEAD_DIM]
            t = t * lax.rsqrt(jnp.mean(t * t, axis=-1, keepdims=True) + EPS) * gain_ref[...]
            if rotary:
                t = (t * cos_ref[...]
                     + pltpu.roll(t, DA_HEAD_DIM - ROPE_FREQS, axis=1) * sa_ref[...]
                     + pltpu.roll(t, ROPE_FREQS, axis=1) * sb_ref[...])
            cols.append(t * scale if scale != 1.0 else t)
        o_ref[...] = jnp.concatenate(cols, axis=1).astype(o_ref.dtype)

    @pl.when(j < n_qk)
    def _():
        qk_norm(qg_ref, DA_SCALE * LOG2E)

    @pl.when(jnp.logical_and(j >= n_qk, j < 2 * n_qk))
    def _():
        qk_norm(kg_ref, 1.0)

    @pl.when(j >= 2 * n_qk)
    def _():
        o_ref[...] = acc.astype(o_ref.dtype)


def _rotary_tables(seq_len):
    t = jnp.arange(seq_len, dtype=jnp.int32)
    row = (t // GRID_W).astype(F32)
    col = (t % GRID_W).astype(F32)
    inv = ROPE_BASE ** (-jnp.arange(ROPE_FREQS, dtype=F32) / ROPE_FREQS)
    ar = row[:, None] * inv
    ac = col[:, None] * inv
    ang = jnp.concatenate([ar, ar, ac, ac], axis=-1)
    cos, sin = jnp.cos(ang), jnp.sin(ang)
    first_half = (jnp.arange(DA_HEAD_DIM) % (2 * ROPE_FREQS)) < ROPE_FREQS
    sin_a = jnp.where(first_half, -sin, 0.0)
    sin_b = jnp.where(first_half, 0.0, sin)
    return cos, sin_a, sin_b


def _qkv(x, mod, rows, seq_len, norm_g, w_bf, q_gain, k_gain, rotary):
    t, d = x.shape
    h = _norm_mod(x, mod, rows, norm_g)
    tm = _tile(seq_len, 1024)
    tn = _tile(d, 512)
    n_qk = d // tn
    tiles_per_seq = seq_len // tm
    cos, sin_a, sin_b = _rotary_tables(seq_len)
    tab_spec = pl.BlockSpec((tm, DA_HEAD_DIM), lambda i, j: (i % tiles_per_seq, 0))
    return pl.pallas_call(
        functools.partial(_qkv_kernel, rotary, n_qk),
        grid=(t // tm, 3 * n_qk),
        in_specs=[pl.BlockSpec((tm, d), lambda i, j: (i, 0)),
                  pl.BlockSpec((d, tn), lambda i, j: (0, j)),
                  pl.BlockSpec((1, DA_HEAD_DIM), lambda i, j: (0, 0)),
                  pl.BlockSpec((1, DA_HEAD_DIM), lambda i, j: (0, 0)),
                  tab_spec, tab_spec, tab_spec],
        out_specs=pl.BlockSpec((tm, tn), lambda i, j: (i, j)),
        out_shape=jax.ShapeDtypeStruct((t, 3 * d), BF16),
        compiler_params=_params(("parallel", "arbitrary")),
        name="qkv_proj",
    )(h, w_bf, q_gain.reshape(1, -1), k_gain.reshape(1, -1), cos, sin_a, sin_b)


def _attn_kernel(lam_init, q_ref, kl_ref, vl_ref, kc_ref, vc_ref, lam_ref, sg_ref, o_ref):
    hd = DA_HEAD_DIM
    lq1, lk1, lq2, lk2 = (lam_ref[r:r + 1, :] for r in range(4))
    lam = (jnp.exp(jnp.sum(lq1 * lk1, axis=-1, keepdims=True))
           - jnp.exp(jnp.sum(lq2 * lk2, axis=-1, keepdims=True)) + lam_init)
    nt = (((1,), (1,)), ((), ()))
    tn = (((0,), (0,)), ((), ()))
    v_lat = vl_ref[...]
    v_ctx = vc_ref[...]

    def attend(sub):
        q = q_ref[:, sub * hd:(sub + 1) * hd]
        s_lat = lax.dot_general(kl_ref[:, sub * hd:(sub + 1) * hd], q, nt, preferred_element_type=F32)
        s_ctx = lax.dot_general(kc_ref[:, sub * hd:(sub + 1) * hd], q, nt, preferred_element_type=F32)
        m = jnp.maximum(jnp.max(s_lat, axis=0, keepdims=True), jnp.max(s_ctx, axis=0, keepdims=True))
        p_lat = jnp.exp2(s_lat - m)
        p_ctx = jnp.exp2(s_ctx - m)
        z = jnp.sum(p_lat, axis=0, keepdims=True) + jnp.sum(p_ctx, axis=0, keepdims=True)
        o = (lax.dot_general(v_lat, p_lat.astype(BF16), tn, preferred_element_type=F32)
             + lax.dot_general(v_ctx, p_ctx.astype(BF16), tn, preferred_element_type=F32))
        return o / z

    ot = attend(0) - lam * attend(1)
    ot = ot * lax.rsqrt(jnp.mean(ot * ot, axis=0, keepdims=True) + EPS)
    o_ref[...] = (ot.T * sg_ref[...] * (1.0 - lam_init)).astype(o_ref.dtype)


def _attention(qkv_lat, qkv_ctx, batch, lam_rows, subln_g, lam_init):
    t, d3 = qkv_lat.shape
    d = d3 // 3
    nh = d // DA_VDIM
    seq = t // batch
    ctx_len = qkv_ctx.shape[0] // batch
    tq = _tile(seq, 512)
    nq = seq // tq
    return pl.pallas_call(
        functools.partial(_attn_kernel, lam_init),
        grid=(batch, nh, nq),
        in_specs=[pl.BlockSpec((tq, DA_VDIM), lambda b, h, i: (b * nq + i, h)),
                  pl.BlockSpec((seq, DA_VDIM), lambda b, h, i: (b, nh + h)),
                  pl.BlockSpec((seq, DA_VDIM), lambda b, h, i: (b, 2 * nh + h)),
                  pl.BlockSpec((ctx_len, DA_VDIM), lambda b, h, i: (b, nh + h)),
                  pl.BlockSpec((ctx_len, DA_VDIM), lambda b, h, i: (b, 2 * nh + h)),
                  pl.BlockSpec((4, DA_HEAD_DIM), lambda b, h, i: (0, 0)),
                  pl.BlockSpec((1, DA_VDIM), lambda b, h, i: (0, 0))],
        out_specs=pl.BlockSpec((tq, DA_VDIM), lambda b, h, i: (b * nq + i, h)),
        out_shape=jax.ShapeDtypeStruct((t, d), BF16),
        compiler_params=_params(("parallel", "parallel", "arbitrary")),
        name="diff_attention",
    )(qkv_lat, qkv_lat, qkv_lat, qkv_ctx, qkv_ctx, lam_rows, subln_g.reshape(1, DA_VDIM))


def _oproj_kernel(a_ref, w_ref, x_ref, gate_ref, o_ref):
    y = jnp.dot(a_ref[...], w_ref[...], preferred_element_type=F32)
    o_ref[...] = x_ref[...] + gate_ref[0] * y


def _oproj(a, w_bf, x, mod, rows):
    t, d = x.shape
    tm = _tile(rows.span, 1024)
    row_of = rows.row_of
    tn = _tile(d, 512)
    nj = d // tn
    return pl.pallas_call(
        _oproj_kernel,
        grid=(t // tm, nj),
        in_specs=[pl.BlockSpec((tm, d), lambda i, j: (i, 0)),
                  pl.BlockSpec((d, tn), lambda i, j: (0, j)),
                  pl.BlockSpec((tm, tn), lambda i, j: (i, j)),
                  pl.BlockSpec((1, 1, tn), lambda i, j: (row_of(i * tm), 0, 2 * nj + j))],
        out_specs=pl.BlockSpec((tm, tn), lambda i, j: (i, j)),
        out_shape=jax.ShapeDtypeStruct((t, d), F32),
        compiler_params=_params(("parallel", "arbitrary")),
        name="attn_out_proj",
    )(a, w_bf, x, mod)


def kernel(x, c, ctx, c_ctx, l0_ada_w, l0_ada_b, l0_norm1_g, l0_glu_w, l0_glu_b, l0_dw_w, l0_dw_b, l0_ln_g, l0_ln_b, l0_pw_w, l0_pw_b, l0_norm2_g, l0_peer_wq, l0_peer_k1, l0_peer_k2, l0_peer_u, l0_peer_v, l1_ada_w, l1_ada_b, l1_norm1_g, l1_qkv_w, l1_q_norm_g, l1_k_norm_g, l1_lam_q1, l1_lam_k1, l1_lam_q2, l1_lam_k2, l1_subln_g, l1_o_w, l1_norm2_g, l1_peer_wq, l1_peer_k1, l1_peer_k2, l1_peer_u, l1_peer_v):
    batch, seq, d = x.shape
    ctx_len = ctx.shape[1]
    assert seq % GRID_W == 0 and d % DA_VDIM == 0 and batch < 8
    xl = x.reshape(batch * seq, d)
    xc = ctx.reshape(batch * ctx_len, d)
    ctx_row = batch
    cvec = jnp.concatenate([c, c_ctx[None, :], jnp.zeros((8 - batch - 1, d), F32)], axis=0)

    lat_row = ModRows(lambda tok: tok // seq, seq)
    ctx_row_of = ModRows(lambda tok: ctx_row, batch * ctx_len)

    bf = lambda w: w.astype(BF16)

    mod0 = _adaln(cvec, l0_ada_w, l0_ada_b)
    glu_bf, pw_bf = bf(l0_glu_w), bf(l0_pw_w)
    peer0 = (bf(l0_peer_wq), bf(l0_peer_k1), bf(l0_peer_k2), bf(l0_peer_u), bf(l0_peer_v))
    conv_args = (l0_dw_w, l0_dw_b, l0_ln_g, l0_ln_b, pw_bf, l0_pw_b)

    def layer0(xs, row_of, seq_len):
        a = _glu(xs, mod0, row_of, l0_norm1_g, glu_bf, l0_glu_b)
        x1 = _conv_pw(a, xs, mod0, row_of, seq_len, *conv_args)
        return _peer_layer(x1, mod0, row_of, l0_norm2_g, *peer0)

    xl = layer0(xl, lat_row, seq)
    xc = layer0(xc, ctx_row_of, ctx_len)

    mod1 = _adaln(cvec, l1_ada_w, l1_ada_b)
    lam_init = 0.8 - 0.6 * math.exp(-0.3 * 1)
    qkv_bf = bf(l1_qkv_w)
    qkv_lat = _qkv(xl, mod1, lat_row, seq, l1_norm1_g, qkv_bf, l1_q_norm_g, l1_k_norm_g, True)
    qkv_ctx = _qkv(xc, mod1, ctx_row_of, ctx_len, l1_norm1_g, qkv_bf, l1_q_norm_g, l1_k_norm_g, False)
    lam_rows = jnp.stack([l1_lam_q1, l1_lam_k1, l1_lam_q2, l1_lam_k2], axis=0)
    o = _attention(qkv_lat, qkv_ctx, batch, lam_rows, l1_subln_g, lam_init)
    xl = _oproj(o, bf(l1_o_w), xl, mod1, lat_row)
    peer1 = (bf(l1_peer_wq), bf(l1_peer_k1), bf(l1_peer_k2), bf(l1_peer_u), bf(l1_peer_v))
    xl = _peer_layer(xl, mod1, lat_row, l1_norm2_g, *peer1)
    return xl.reshape(batch, seq, d)
```
